```python
import jax, jax.numpy as jnp
from jax import lax
import numpy as np

D_MODEL = 2048
BATCH = 4
SEQ = 4096
DEPTH = 2

GRID_W = 64
N_MEM = 256
HEAD_DIM = 128
N_Q_HEADS = D_MODEL // HEAD_DIM
N_KV_HEADS = N_Q_HEADS // 4
GROUP = N_Q_HEADS // N_KV_HEADS
ATTN_W = N_Q_HEADS * HEAD_DIM
KV_W = N_KV_HEADS * HEAD_DIM
ROPE_HALF = HEAD_DIM // 2
ROPE_THETA = 10000.0
Q_BLOCK = 128
D_RNN = D_MODEL
N_RNN_BLOCKS = 16
RNN_BLOCK = D_RNN // N_RNN_BLOCKS
CONV_W = 4
CONV_LEFT = CONV_W // 2
LRU_C = 8.0
N_XHEADS = 4
XHEAD_DIM = D_MODEL // N_XHEADS
D_FF = 4 * D_MODEL
N_IN = ATTN_W + 2 * KV_W + 2 * D_RNN + 2 * D_MODEL
SPLITS = [ATTN_W, ATTN_W + KV_W, ATTN_W + 2 * KV_W, ATTN_W + 2 * KV_W + D_RNN,
          ATTN_W + 2 * KV_W + 2 * D_RNN, ATTN_W + 2 * KV_W + 2 * D_RNN + D_MODEL]
EPS = 1e-6

kernel_name = 'hybrid_gqa_rglru_xattn_encoder'


def rmsnorm(x, g):
    xf = x.astype(jnp.float32)
    y = xf * lax.rsqrt(jnp.mean(xf * xf, axis=-1, keepdims=True) + EPS)
    return (y * g.astype(jnp.float32)).astype(x.dtype)


def axial_rope_tables(seq_len):
    rows_n = seq_len // GRID_W
    row = jnp.repeat(jnp.arange(rows_n, dtype=jnp.float32), GRID_W)
    col = jnp.tile(jnp.arange(GRID_W, dtype=jnp.float32), rows_n)
    n_freq = ROPE_HALF // 2
    inv = ROPE_THETA ** (-jnp.arange(n_freq, dtype=jnp.float32) / n_freq)
    ang_r = row[:, None] * inv[None, :]
    ang_c = col[:, None] * inv[None, :]
    return (jnp.cos(ang_r), jnp.sin(ang_r), jnp.cos(ang_c), jnp.sin(ang_c))


def _rotate(x, cos, sin):
    n = x.shape[-1] // 2
    x1, x2 = x[..., :n], x[..., n:]
    c = cos[None, :, None, :]
    s = sin[None, :, None, :]
    return jnp.concatenate([x1 * c - x2 * s, x2 * c + x1 * s], axis=-1)


def head_norm_axial_rope(x, g, tabs):
    cr, sr, cc, sc = tabs
    xf = x.astype(jnp.float32)
    xf = xf * lax.rsqrt(jnp.mean(xf * xf, axis=-1, keepdims=True) + EPS) * g.astype(jnp.float32)
    out = jnp.concatenate([_rotate(xf[..., :ROPE_HALF], cr, sr),
                           _rotate(xf[..., ROPE_HALF:], cc, sc)], axis=-1)
    return out.astype(x.dtype)


def block_gqa(q, k, v):
    B, S = q.shape[0], q.shape[1]
    nb = S // Q_BLOCK
    qb = q.reshape(B, nb, Q_BLOCK, N_KV_HEADS, GROUP, HEAD_DIM).transpose(1, 0, 2, 3, 4, 5)
    scale = HEAD_DIM ** -0.5

    def one_block(qblk):
        s = jnp.einsum('bqkgd,bskd->bkgqs', qblk, k, preferred_element_type=jnp.float32) * scale
        p = jax.nn.softmax(s, axis=-1).astype(v.dtype)
        return jnp.einsum('bkgqs,bskd->bqkgd', p, v)

    o = lax.map(one_block, qb)
    return o.transpose(1, 0, 2, 3, 4, 5).reshape(B, S, ATTN_W)


def centred_depthwise_conv(u, w, b):
    S = u.shape[1]
    up = jnp.pad(u, ((0, 0), (CONV_LEFT, CONV_W - 1 - CONV_LEFT), (0, 0)))
    out = b[None, None, :]
    for tap in range(CONV_W):
        out = out + up[:, tap:tap + S, :] * w[tap][None, None, :]
    return out


def _lru_combine(e1, e2):
    a1, b1 = e1
    a2, b2 = e2
    return a1 * a2, a2 * b1 + b2


def rglru_direction(u, w_r, b_r, w_i, b_i, lam, reverse):
    B, S, _ = u.shape
    ub = u.reshape(B, S, N_RNN_BLOCKS, RNN_BLOCK)
    r = jax.nn.sigmoid(jnp.einsum('bsnc,ncd->bsnd', ub, w_r.astype(jnp.float32)).reshape(B, S, D_RNN)
                       + b_r.astype(jnp.float32))
    i = jax.nn.sigmoid(jnp.einsum('bsnc,ncd->bsnd', ub, w_i.astype(jnp.float32)).reshape(B, S, D_RNN)
                       + b_i.astype(jnp.float32))
    log_a = -LRU_C * r * jax.nn.softplus(-lam.astype(jnp.float32))
    a = jnp.exp(log_a)
    bterm = jnp.sqrt(-jnp.expm1(2.0 * log_a)) * (i * u)
    if reverse:
        a = jnp.flip(a, axis=1)
        bterm = jnp.flip(bterm, axis=1)
    _, h = lax.associative_scan(_lru_combine, (a, bterm), axis=1)
    if reverse:
        h = jnp.flip(h, axis=1)
    return h


def setup_inputs(seed: int = 0) -> dict:
    key = jax.random.key(seed)
    ks = jax.random.split(key, 32)
    f32 = jnp.float32

    def nrm(k, shape, fan_in):
        return jax.random.normal(k, shape, f32) * (fan_in ** -0.5)

    def gain(k, shape):
        return 1.0 + 0.02 * jax.random.normal(k, shape, f32)

    u = jax.random.uniform(ks[12], (DEPTH, 2, D_RNN), f32, 0.9, 0.999)
    a0 = u ** (1.0 / LRU_C)
    lam = jnp.log(a0) - jnp.log1p(-a0)
    return {
        'x': jax.random.normal(ks[0], (BATCH, SEQ, D_MODEL), f32),
        'mem': jax.random.normal(ks[1], (BATCH, N_MEM, D_MODEL), f32),
        'mix_norm_g': gain(ks[2], (DEPTH, D_MODEL)),
        'w_in': nrm(ks[3], (DEPTH, D_MODEL, N_IN), D_MODEL),
        'q_norm_g': gain(ks[4], (DEPTH, HEAD_DIM)),
        'k_norm_g': gain(ks[5], (DEPTH, HEAD_DIM)),
        'conv_w': nrm(ks[6], (DEPTH, CONV_W, D_RNN), CONV_W),
        'conv_b': 0.01 * jax.random.normal(ks[7], (DEPTH, D_RNN), f32),
        'lru_w_r': nrm(ks[8], (DEPTH, 2, N_RNN_BLOCKS, RNN_BLOCK, RNN_BLOCK), RNN_BLOCK),
        'lru_b_r': 0.01 * jax.random.normal(ks[9], (DEPTH, 2, D_RNN), f32),
        'lru_w_i': nrm(ks[10], (DEPTH, 2, N_RNN_BLOCKS, RNN_BLOCK, RNN_BLOCK), RNN_BLOCK),
        'lru_b_i': 0.01 * jax.random.normal(ks[11], (DEPTH, 2, D_RNN), f32),
        'lru_lambda': lam,
        'w_attn_branch': nrm(ks[13], (DEPTH, ATTN_W, D_MODEL), ATTN_W),
        'w_rnn_branch': nrm(ks[14], (DEPTH, D_RNN, D_MODEL), D_RNN),
        'w_mix_out': nrm(ks[15], (DEPTH, D_MODEL, D_MODEL), D_MODEL),
        'cross_norm_g': gain(ks[16], (DEPTH, D_MODEL)),
        'mem_norm_g': gain(ks[17], (DEPTH, D_MODEL)),
        'w_xq': nrm(ks[18], (DEPTH, D_MODEL, D_MODEL), D_MODEL),
        'w_xkv': nrm(ks[19], (DEPTH, D_MODEL, 2 * D_MODEL), D_MODEL),
        'w_xo': nrm(ks[20], (DEPTH, D_MODEL, D_MODEL), D_MODEL),
        'mlp_norm_g': gain(ks[21], (DEPTH, D_MODEL)),
        'w_up': nrm(ks[22], (DEPTH, D_MODEL, D_FF), D_MODEL),
        'w_down': nrm(ks[23], (DEPTH, D_FF, D_MODEL), D_FF),
        'final_norm_g': gain(ks[24], (D_MODEL,)),
    }


def reference(x, mem, mix_norm_g, w_in, q_norm_g, k_norm_g, conv_w, conv_b, lru_w_r, lru_b_r,
              lru_w_i, lru_b_i, lru_lambda, w_attn_branch, w_rnn_branch, w_mix_out, cross_norm_g,
              mem_norm_g, w_xq, w_xkv, w_xo, mlp_norm_g, w_up, w_down, final_norm_g):
    B, S, _ = x.shape
    M = mem.shape[1]
    dt = x.dtype
    tabs = axial_rope_tables(S)
    for l in range(DEPTH):
        h = rmsnorm(x, mix_norm_g[l])
        proj = h @ w_in[l]
        q, k, v, u, y, g_a, g_r = jnp.split(proj, SPLITS, axis=-1)
        q = head_norm_axial_rope(q.reshape(B, S, N_Q_HEADS, HEAD_DIM), q_norm_g[l], tabs)
        k = head_norm_axial_rope(k.reshape(B, S, N_KV_HEADS, HEAD_DIM), k_norm_g[l], tabs)
        v = v.reshape(B, S, N_KV_HEADS, HEAD_DIM)
        o_attn = block_gqa(q, k, v)
        uf = centred_depthwise_conv(u.astype(jnp.float32), conv_w[l].astype(jnp.float32),
                                    conv_b[l].astype(jnp.float32))
        h_fwd = rglru_direction(uf, lru_w_r[l, 0], lru_b_r[l, 0], lru_w_i[l, 0], lru_b_i[l, 0],
                                lru_lambda[l, 0], reverse=False)
        h_bwd = rglru_direction(uf, lru_w_r[l, 1], lru_b_r[l, 1], lru_w_i[l, 1], lru_b_i[l, 1],
                                lru_lambda[l, 1], reverse=True)
        o_rnn = ((h_fwd + h_bwd) * jax.nn.gelu(y.astype(jnp.float32))).astype(dt)
        merged = (jax.nn.sigmoid(g_a) * (o_attn @ w_attn_branch[l])
                  + jax.nn.sigmoid(g_r) * (o_rnn @ w_rnn_branch[l]))
        x = x + merged @ w_mix_out[l]
        hc = rmsnorm(x, cross_norm_g[l])
        mn = rmsnorm(mem, mem_norm_g[l])
        xq = (hc @ w_xq[l]).reshape(B, S, N_XHEADS, XHEAD_DIM)
        xk, xv = jnp.split(mn @ w_xkv[l], 2, axis=-1)
        xk = xk.reshape(B, M, N_XHEADS, XHEAD_DIM)
        xv = xv.reshape(B, M, N_XHEADS, XHEAD_DIM)
        s = jnp.einsum('bshd,bmhd->bhsm', xq, xk, preferred_element_type=jnp.float32) * (XHEAD_DIM ** -0.5)
        p = jax.nn.softmax(s, axis=-1).astype(dt)
        xo = jnp.einsum('bhsm,bmhd->bshd', p, xv).reshape(B, S, D_MODEL)
        x = x + xo @ w_xo[l]
        hm = rmsnorm(x, mlp_norm_g[l])
        x = x + jnp.square(jax.nn.relu(hm @ w_up[l])) @ w_down[l]
    return rmsnorm(x, final_norm_g)
```

```python
import functools
import math

import jax
import jax.numpy as jnp
from jax import lax
from jax.experimental import pallas as pl
from jax.experimental.pallas import tpu as pltpu

F32 = jnp.float32
BF16 = jnp.bfloat16

GRID_W = 64
HEAD_DIM = 128
KV_GROUP = 4
ROPE_THETA = 10000.0
CONV_W = 4
CONV_LEFT = CONV_W // 2
LRU_C = 8.0
N_XHEADS = 4
EPS = 1e-6
LOG2E = math.log2(math.e)

LANES = 128
SUBLANES = 8
VMEM_LIMIT_BYTES = 56 * 1024 * 1024


def _params(*sem):
    return pltpu.CompilerParams(dimension_semantics=sem, vmem_limit_bytes=VMEM_LIMIT_BYTES)


def _rms_scale(x):
    return lax.rsqrt(jnp.mean(x * x, axis=-1, keepdims=True) + EPS)


def _norm_matmul_kernel(x_ref, g_ref, w_ref, o_ref, h_ref, *, relu2):
    @pl.when(pl.program_id(1) == 0)
    def _():
        x = x_ref[...]
        h_ref[...] = (x * _rms_scale(x) * g_ref[...]).astype(BF16)

    y = jnp.dot(h_ref[...], w_ref[...], preferred_element_type=F32)
    if relu2:
        y = jnp.square(jnp.maximum(y, 0.0))
    o_ref[...] = y.astype(o_ref.dtype)


def norm_matmul(x, g, w, *, out_dtype, tm, tn, relu2=False, name):
    m, d = x.shape
    n = w.shape[1]
    return pl.pallas_call(
        functools.partial(_norm_matmul_kernel, relu2=relu2),
        grid=(m // tm, n // tn),
        in_specs=[
            pl.BlockSpec((tm, d), lambda i, j: (i, 0)),
            pl.BlockSpec((1, d), lambda i, j: (0, 0)),
            pl.BlockSpec((d, tn), lambda i, j: (0, j)),
        ],
        out_specs=pl.BlockSpec((tm, tn), lambda i, j: (i, j)),
        out_shape=jax.ShapeDtypeStruct((m, n), out_dtype),
        scratch_shapes=[pltpu.VMEM((tm, d), BF16)],
        compiler_params=_params("parallel", "arbitrary"),
        name=name,
    )(x, g.reshape(1, d), w)


def _matmul_residual_kernel(a_ref, w_ref, x_ref, o_ref):
    o_ref[...] = x_ref[...] + jnp.dot(a_ref[...], w_ref[...], preferred_element_type=F32)


def matmul_residual(a, w, x, *, tm, tn, name):
    m, k = a.shape
    n = w.shape[1]
    return pl.pallas_call(
        _matmul_residual_kernel,
        grid=(m // tm, n // tn),
        in_specs=[
            pl.BlockSpec((tm, k), lambda i, j: (i, 0)),
            pl.BlockSpec((k, tn), lambda i, j: (0, j)),
            pl.BlockSpec((tm, tn), lambda i, j: (i, j)),
        ],
        out_specs=pl.BlockSpec((tm, tn), lambda i, j: (i, j)),
        out_shape=jax.ShapeDtypeStruct((m, n), F32),
        compiler_params=_params("parallel", "arbitrary"),
        name=name,
    )(a, w, x)


def _merge_kernel(oa_ref, or_ref, wa_ref, wr_ref, ga_ref, gr_ref, o_ref):
    ya = jnp.dot(oa_ref[...], wa_ref[...], preferred_element_type=F32)
    yr = jnp.dot(or_ref[...], wr_ref[...], preferred_element_type=F32)
    o_ref[...] = (jax.nn.sigmoid(ga_ref[...]) * ya + jax.nn.sigmoid(gr_ref[...]) * yr).astype(o_ref.dtype)


def merge_branches(o_attn, o_rnn, w_a, w_r, proj, ga_col, gr_col, *, tm, tn, name):
    m, k = o_attn.shape
    n = w_a.shape[1]
    ga_blk, gr_blk = ga_col // tn, gr_col // tn
    return pl.pallas_call(
        _merge_kernel,
        grid=(m // tm, n // tn),
        in_specs=[
            pl.BlockSpec((tm, k), lambda i, j: (i, 0)),
            pl.BlockSpec((tm, k), lambda i, j: (i, 0)),
            pl.BlockSpec((k, tn), lambda i, j: (0, j)),
            pl.BlockSpec((k, tn), lambda i, j: (0, j)),
            pl.BlockSpec((tm, tn), lambda i, j: (i, ga_blk + j)),
            pl.BlockSpec((tm, tn), lambda i, j: (i, gr_blk + j)),
        ],
        out_specs=pl.BlockSpec((tm, tn), lambda i, j: (i, j)),
        out_shape=jax.ShapeDtypeStruct((m, n), BF16),
        compiler_params=_params("parallel", "arbitrary"),
        name=name,
    )(o_attn, o_rnn, w_a, w_r, proj, proj)


def _rope_kernel(x_ref, cos_ref, sin_ref, gq_ref, gk_ref, o_ref, *, n_q_blocks, q_scale):
    is_q = pl.program_id(1) < n_q_blocks
    g = jnp.where(is_q, gq_ref[...] * q_scale, gk_ref[...])
    cos = cos_ref[...]
    sin = sin_ref[...]
    lane = lax.broadcasted_iota(jnp.int32, cos.shape, 1)
    first_half = (lane % (HEAD_DIM // 2)) < (HEAD_DIM // 4)
    for h in range(x_ref.shape[1] // HEAD_DIM):
        sl = slice(h * HEAD_DIM, (h + 1) * HEAD_DIM)
        x = x_ref[:, sl]
        xn = x * _rms_scale(x) * g
        swapped = jnp.where(first_half,
                            pltpu.roll(xn, HEAD_DIM - HEAD_DIM // 4, axis=1),
                            pltpu.roll(xn, HEAD_DIM // 4, axis=1))
        o_ref[:, sl] = (xn * cos + swapped * sin).astype(o_ref.dtype)


def rope_qk(proj, cos_t, sin_t, gq, gk, *, seq, n_qk_cols, n_q_cols, tm, tn, name):
    m = proj.shape[0]
    t_blocks = seq // tm
    return pl.pallas_call(
        functools.partial(_rope_kernel, n_q_blocks=n_q_cols // tn,
                          q_scale=HEAD_DIM ** -0.5 * LOG2E),
        grid=(m // tm, n_qk_cols // tn),
        in_specs=[
            pl.BlockSpec((tm, tn), lambda i, j: (i, j)),
            pl.BlockSpec((tm, HEAD_DIM), lambda i, j: (i % t_blocks, 0)),
            pl.BlockSpec((tm, HEAD_DIM), lambda i, j: (i % t_blocks, 0)),
            pl.BlockSpec((1, HEAD_DIM), lambda i, j: (0, 0)),
            pl.BlockSpec((1, HEAD_DIM), lambda i, j: (0, 0)),
        ],
        out_specs=pl.BlockSpec((tm, tn), lambda i, j: (i, j)),
        out_shape=jax.ShapeDtypeStruct((m, n_qk_cols), BF16),
        compiler_params=_params("parallel", "parallel"),
        name=name,
    )(proj, cos_t, sin_t, gq.reshape(1, HEAD_DIM), gk.reshape(1, HEAD_DIM))


def rope_tables(seq):
    rows_n = seq // GRID_W
    row = jnp.repeat(jnp.arange(rows_n, dtype=F32), GRID_W)
    col = jnp.tile(jnp.arange(GRID_W, dtype=F32), rows_n)
    n_freq = HEAD_DIM // 4
    inv = ROPE_THETA ** (-jnp.arange(n_freq, dtype=F32) / n_freq)
    ang_r = row[:, None] * inv[None, :]
    ang_c = col[:, None] * inv[None, :]
    cos_t = jnp.concatenate([jnp.cos(ang_r), jnp.cos(ang_r), jnp.cos(ang_c), jnp.cos(ang_c)], axis=-1)
    sin_t = jnp.concatenate([-jnp.sin(ang_r), jnp.sin(ang_r), -jnp.sin(ang_c), jnp.sin(ang_c)], axis=-1)
    return cos_t, sin_t


def _attn_kernel(q_ref, k_ref, v_ref, o_ref, s_ref, vb_ref, *, kc):
    tq = q_ref.shape[0]
    seq = k_ref.shape[0]
    rows = KV_GROUP * tq

    @pl.when(pl.program_id(2) == 0)
    def _():
        vb_ref[...] = v_ref[...].astype(BF16)

    q = jnp.concatenate([q_ref[:, g * HEAD_DIM:(g + 1) * HEAD_DIM] for g in range(KV_GROUP)], axis=0)

    m_part = jnp.full((rows, LANES), -jnp.inf, F32)
    for c in range(seq // kc):
        s = lax.dot_general(q, k_ref[c * kc:(c + 1) * kc, :], (((1,), (1,)), ((), ())),
                            preferred_element_type=F32)
        s_ref[:, c * kc:(c + 1) * kc] = s
        for l in range(kc // LANES):
            m_part = jnp.maximum(m_part, s[:, l * LANES:(l + 1) * LANES])
    m = jnp.broadcast_to(jnp.max(m_part, axis=1, keepdims=True), (rows, LANES))

    l_part = jnp.zeros((rows, LANES), F32)
    acc = jnp.zeros((rows, HEAD_DIM), F32)
    for c in range(seq // kc):
        ps = []
        for l in range(kc // LANES):
            p = jnp.exp2(s_ref[:, c * kc + l * LANES:c * kc + (l + 1) * LANES] - m)
            l_part = l_part + p
            ps.append(p.astype(BF16))
        acc = acc + jnp.dot(jnp.concatenate(ps, axis=1), vb_ref[c * kc:(c + 1) * kc, :],
                            preferred_element_type=F32)
    l_sum = jnp.broadcast_to(jnp.sum(l_part, axis=1, keepdims=True), (rows, LANES))
    out = acc / l_sum
    for g in range(KV_GROUP):
        o_ref[:, g * HEAD_DIM:(g + 1) * HEAD_DIM] = out[g * tq:(g + 1) * tq].astype(o_ref.dtype)


def gqa_attention(qk, proj, *, batch, seq, n_q_cols, v_col, tq, kc, name):
    n_kv = n_q_cols // (KV_GROUP * HEAD_DIM)
    group_w = KV_GROUP * HEAD_DIM
    qk3 = qk.reshape(batch, seq, qk.shape[1])
    proj3 = proj.reshape(batch, seq, proj.shape[1])
    k_blk0 = n_q_cols // HEAD_DIM
    v_blk0 = v_col // HEAD_DIM
    out = pl.pallas_call(
        functools.partial(_attn_kernel, kc=kc),
        grid=(batch, n_kv, seq // tq),
        in_specs=[
            pl.BlockSpec((None, tq, group_w), lambda b, j, i: (b, i, j)),
            pl.BlockSpec((None, seq, HEAD_DIM), lambda b, j, i: (b, 0, k_blk0 + j)),
            pl.BlockSpec((None, seq, HEAD_DIM), lambda b, j, i: (b, 0, v_blk0 + j)),
        ],
        out_specs=pl.BlockSpec((None, tq, group_w), lambda b, j, i: (b, i, j)),
        out_shape=jax.ShapeDtypeStruct((batch, seq, n_q_cols), BF16),
        scratch_shapes=[pltpu.VMEM((KV_GROUP * tq, seq), F32), pltpu.VMEM((seq, HEAD_DIM), BF16)],
        compiler_params=_params("parallel", "parallel", "arbitrary"),
        name=name,
    )(qk3, qk3, proj3)
    return out.reshape(batch * seq, n_q_cols)


def _softplus(x):
    e = jnp.exp(-jnp.abs(x))
    u = 1.0 + e
    log1p_e = jnp.where(u == 1.0, e, jnp.log(u) * e / (u - 1.0))
    return jnp.maximum(x, 0.0) + log1p_e


def _scan8(a, b, reverse):
    row = lax.broadcasted_iota(jnp.int32, a.shape, 0)
    for d in (1, 2, 4):
        if reverse:
            valid = row < SUBLANES - d
            shift = SUBLANES - d
        else:
            valid = row >= d
            shift = d
        a_prev = jnp.where(valid, pltpu.roll(a, shift, axis=0), 1.0)
        b_prev = jnp.where(valid, pltpu.roll(b, shift, axis=0), 0.0)
        b = a * b_prev + b
        a = a * a_prev
    return a, b


def _lru_kernel(u_ref, y_ref, cw_ref, cb_ref, wg_ref, bg_ref, lam_ref, o_ref,
                upad_ref, af_ref, bf_ref, ab_ref, bb_ref, h_ref, *, tc, unroll):
    seq, cb = u_ref.shape
    pad = SUBLANES

    zeros = jnp.zeros((pad, cb), F32)
    upad_ref[0:pad, :] = zeros
    upad_ref[pad + seq:2 * pad + seq, :] = zeros
    upad_ref[pad:pad + seq, :] = u_ref[...]

    decay_f = -LRU_C * _softplus(-lam_ref[0:1, :])
    decay_b = -LRU_C * _softplus(-lam_ref[1:2, :])

    for c in range(seq // tc):
        t0 = c * tc
        uf = jnp.broadcast_to(cb_ref[...], (tc, cb))
        for tap in range(CONV_W):
            start = pad + t0 + tap - CONV_LEFT
            uf = uf + upad_ref[start:start + tc, :] * cw_ref[tap:tap + 1, :]
        gates = jnp.dot(uf.astype(BF16), wg_ref[...], preferred_element_type=F32) + bg_ref[...]
        for d, (decay, a_ref, b_ref) in enumerate(((decay_f, af_ref, bf_ref), (decay_b, ab_ref, bb_ref))):
            r = jax.nn.sigmoid(gates[:, (2 * d) * cb:(2 * d + 1) * cb])
            i = jax.nn.sigmoid(gates[:, (2 * d + 1) * cb:(2 * d + 2) * cb])
            a = jnp.exp(decay * r)
            a_ref[t0:t0 + tc, :] = a
            b_ref[t0:t0 + tc, :] = jnp.sqrt(1.0 - a * a) * (i * uf)

    n_tiles = seq // SUBLANES

    def fwd_body(j, carry):
        t = pl.multiple_of(j * SUBLANES, SUBLANES)
        a, b = _scan8(af_ref[pl.ds(t, SUBLANES), :], bf_ref[pl.ds(t, SUBLANES), :], reverse=False)
        h = a * carry + b
        h_ref[pl.ds(t, SUBLANES), :] = h
        return jnp.broadcast_to(h[SUBLANES - 1:SUBLANES, :], h.shape)

    lax.fori_loop(0, n_tiles, fwd_body, jnp.zeros((SUBLANES, cb), F32), unroll=unroll)

    def bwd_body(j, carry):
        t = pl.multiple_of((n_tiles - 1 - j) * SUBLANES, SUBLANES)
        a, b = _scan8(ab_ref[pl.ds(t, SUBLANES), :], bb_ref[pl.ds(t, SUBLANES), :], reverse=True)
        h = a * carry + b
        h_ref[pl.ds(t, SUBLANES), :] = h_ref[pl.ds(t, SUBLANES), :] + h
        return jnp.broadcast_to(h[0:1, :], h.shape)

    lax.fori_loop(0, n_tiles, bwd_body, jnp.zeros((SUBLANES, cb), F32), unroll=unroll)

    for c in range(seq // tc):
        sl = slice(c * tc, (c + 1) * tc)
        o_ref[sl, :] = (h_ref[sl, :] * jax.nn.gelu(y_ref[sl, :])).astype(o_ref.dtype)


def rglru_branch(proj, conv_w, conv_b, w_gates, b_gates, lam, *, batch, seq, u_col, y_col, d_rnn, name):
    cb = LANES
    n_blk = d_rnn // cb
    proj3 = proj.reshape(batch, seq, proj.shape[1])
    u_blk0, y_blk0 = u_col // cb, y_col // cb
    out = pl.pallas_call(
        functools.partial(_lru_kernel, tc=256, unroll=4),
        grid=(batch, n_blk),
        in_specs=[
            pl.BlockSpec((None, seq, cb), lambda b, n: (b, 0, u_blk0 + n)),
            pl.BlockSpec((None, seq, cb), lambda b, n: (b, 0, y_blk0 + n)),
            pl.BlockSpec((CONV_W, cb), lambda b, n: (0, n)),
            pl.BlockSpec((1, cb), lambda b, n: (0, n)),
            pl.BlockSpec((None, cb, 4 * cb), lambda b, n: (n, 0, 0)),
            pl.BlockSpec((None, 1, 4 * cb), lambda b, n: (n, 0, 0)),
            pl.BlockSpec((2, cb), lambda b, n: (0, n)),
        ],
        out_specs=pl.BlockSpec((None, seq, cb), lambda b, n: (b, 0, n)),
        out_shape=jax.ShapeDtypeStruct((batch, seq, d_rnn), BF16),
        scratch_shapes=[pltpu.VMEM((seq + 2 * SUBLANES, cb), F32)] + [pltpu.VMEM((seq, cb), F32)] * 5,
        compiler_params=_params("parallel", "parallel"),
        name=name,
    )(proj3, proj3, conv_w, conv_b.reshape(1, d_rnn), w_gates, b_gates, lam)
    return out.reshape(batch * seq, d_rnn)


def _xattn_kernel(q_ref, kv_ref, o_ref, *, n_heads):
    d_model = q_ref.shape[1]
    hd = d_model // n_heads
    scale = hd ** -0.5 * LOG2E
    for h in range(n_heads):
        q = q_ref[:, h * hd:(h + 1) * hd]
        k = kv_ref[:, h * hd:(h + 1) * hd]
        v = kv_ref[:, d_model + h * hd:d_model + (h + 1) * hd]
        s = lax.dot_general(q, k, (((1,), (1,)), ((), ())), preferred_element_type=F32) * scale
        p = jnp.exp2(s - jnp.max(s, axis=-1, keepdims=True))
        o = jnp.dot(p.astype(BF16), v, preferred_element_type=F32) / jnp.sum(p, axis=-1, keepdims=True)
        o_ref[:, h * hd:(h + 1) * hd] = o.astype(o_ref.dtype)


def cross_attention(xq, kv, *, batch, seq, tq, name):
    d_model = xq.shape[1]
    n_mem = kv.shape[0] // batch
    out = pl.pallas_call(
        functools.partial(_xattn_kernel, n_heads=N_XHEADS),
        grid=(batch, seq // tq),
        in_specs=[
            pl.BlockSpec((None, tq, d_model), lambda b, i: (b, i, 0)),
            pl.BlockSpec((None, n_mem, 2 * d_model), lambda b, i: (b, 0, 0)),
        ],
        out_specs=pl.BlockSpec((None, tq, d_model), lambda b, i: (b, i, 0)),
        out_shape=jax.ShapeDtypeStruct((batch, seq, d_model), BF16),
        compiler_params=_params("parallel", "parallel"),
        name=name,
    )(xq.reshape(batch, seq, d_model), kv.reshape(batch, n_mem, 2 * d_model))
    return out.reshape(batch * seq, d_model)


def _rmsnorm_kernel(x_ref, g_ref, o_ref):
    x = x_ref[...]
    o_ref[...] = x * _rms_scale(x) * g_ref[...]


def rmsnorm(x, g, *, tm, name):
    m, d = x.shape
    return pl.pallas_call(
        _rmsnorm_kernel,
        grid=(m // tm,),
        in_specs=[pl.BlockSpec((tm, d), lambda i: (i, 0)), pl.BlockSpec((1, d), lambda i: (0, 0))],
        out_specs=pl.BlockSpec((tm, d), lambda i: (i, 0)),
        out_shape=jax.ShapeDtypeStruct((m, d), F32),
        compiler_params=_params("parallel"),
        name=name,
    )(x, g.reshape(1, d))


def kernel(x, mem, mix_norm_g, w_in, q_norm_g, k_norm_g, conv_w, conv_b, lru_w_r, lru_b_r, lru_w_i, lru_b_i, lru_lambda, w_attn_branch, w_rnn_branch, w_mix_out, cross_norm_g, mem_norm_g, w_xq, w_xkv, w_xo, mlp_norm_g, w_up, w_down, final_norm_g):
    batch, seq, d_model = x.shape
    n_mem = mem.shape[1]
    depth = w_in.shape[0]
    attn_w = w_attn_branch.shape[1]
    d_rnn = w_rnn_branch.shape[1]
    kv_w = attn_w // KV_GROUP
    n_blocks, rnn_block = lru_w_r.shape[2], lru_w_r.shape[3]
    k_col = attn_w
    v_col = k_col + kv_w
    u_col = v_col + kv_w
    y_col = u_col + d_rnn
    ga_col = y_col + d_rnn
    gr_col = ga_col + d_model
    assert w_in.shape[2] == gr_col + d_model and rnn_block == LANES

    xf = x.reshape(batch * seq, d_model)
    memf = mem.reshape(batch * n_mem, d_model)
    cos_t, sin_t = rope_tables(seq)

    for l in range(depth):
        tag = f"l{l}_"
        proj = norm_matmul(xf, mix_norm_g[l], w_in[l].astype(BF16), out_dtype=F32, tm=1024, tn=1024,
                           name=tag + "in_proj")
        qk = rope_qk(proj, cos_t, sin_t, q_norm_g[l], k_norm_g[l], seq=seq, n_qk_cols=v_col,
                     n_q_cols=attn_w, tm=512, tn=512, name=tag + "rope")
        o_attn = gqa_attention(qk, proj, batch=batch, seq=seq, n_q_cols=attn_w, v_col=v_col,
                               tq=256, kc=512, name=tag + "attn")
        w_gates = jnp.concatenate([lru_w_r[l, 0], lru_w_i[l, 0], lru_w_r[l, 1], lru_w_i[l, 1]],
                                  axis=-1).astype(BF16)
        b_gates = jnp.concatenate(
            [b.reshape(n_blocks, 1, rnn_block) for b in (lru_b_r[l, 0], lru_b_i[l, 0], lru_b_r[l, 1], lru_b_i[l, 1])],
            axis=-1)
        o_rnn = rglru_branch(proj, conv_w[l], conv_b[l], w_gates, b_gates, lru_lambda[l], batch=batch, seq=seq,
                             u_col=u_col, y_col=y_col, d_rnn=d_rnn, name=tag + "rglru")
        merged = merge_branches(o_attn, o_rnn, w_attn_branch[l].astype(BF16), w_rnn_branch[l].astype(BF16),
                                proj, ga_col, gr_col, tm=1024, tn=512, name=tag + "merge")
        xf = matmul_residual(merged, w_mix_out[l].astype(BF16), xf, tm=1024, tn=1024, name=tag + "mix_out")
        xq = norm_matmul(xf, cross_norm_g[l], w_xq[l].astype(BF16), out_dtype=BF16, tm=1024, tn=1024,
                         name=tag + "xq")
        kv = norm_matmul(memf, mem_norm_g[l], w_xkv[l].astype(BF16), out_dtype=BF16, tm=batch * n_mem, tn=1024,
                         name=tag + "xkv")
        xo = cross_attention(xq, kv, batch=batch, seq=seq, tq=512, name=tag + "xattn")
        xf = matmul_residual(xo, w_xo[l].astype(BF16), xf, tm=1024, tn=1024, name=tag + "xo")
        hidden = norm_matmul(xf, mlp_norm_g[l], w_up[l].astype(BF16), out_dtype=BF16, tm=1024, tn=1024,
                             relu2=True, name=tag + "mlp_up")
        xf = matmul_residual(hidden, w_down[l].astype(BF16), xf, tm=512, tn=512, name=tag + "mlp_down")
    out = rmsnorm(xf, final_norm_g, tm=512, name="final_norm")
    return out.reshape(batch, seq, d_model)
```

```python
import functools
import math

import jax
import jax.numpy as jnp
from jax import lax
from jax.experimental import pallas as pl
from jax.experimental.pallas import tpu as pltpu

F32 = jnp.float32
BF16 = jnp.bfloat16

GRID_W = 64
HEAD_DIM = 128
KV_GROUP = 4
ROPE_THETA = 10000.0
CONV_W = 4
CONV_LEFT = CONV_W // 2
LRU_C = 8.0
N_XHEADS = 4
EPS = 1e-6
LOG2E = math.log2(math.e)

LANES = 128
SUBLANES = 8
VMEM_LIMIT_BYTES = 56 * 1024 * 1024


def _params(*sem):
    return pltpu.CompilerParams(dimension_semantics=sem, vmem_limit_bytes=VMEM_LIMIT_BYTES)


def _rms_scale(x):
    return lax.rsqrt(jnp.mean(x * x, axis=-1, keepdims=True) + EPS)


def _norm_matmul_kernel(x_ref, g_ref, w_ref, o_ref, h_ref, *, relu2):
    @pl.when(pl.program_id(1) == 0)
    def _():
        x = x_ref[...]
        h_ref[...] = (x * _rms_scale(x) * g_ref[...]).astype(BF16)

    y = jnp.dot(h_ref[...], w_ref[...], preferred_element_type=F32)
    if relu2:
        y = jnp.square(jnp.maximum(y, 0.0))
    o_ref[...] = y.astype(o_ref.dtype)


def norm_matmul(x, g, w, *, out_dtype, tm, tn, relu2=False, name):
    m, d = x.shape
    n = w.shape[1]
    return pl.pallas_call(
        functools.partial(_norm_matmul_kernel, relu2=relu2),
        grid=(m // tm, n // tn),
        in_specs=[
            pl.BlockSpec((tm, d), lambda i, j: (i, 0)),
            pl.BlockSpec((1, d), lambda i, j: (0, 0)),
            pl.BlockSpec((d, tn), lambda i, j: (0, j)),
        ],
        out_specs=pl.BlockSpec((tm, tn), lambda i, j: (i, j)),
        out_shape=jax.ShapeDtypeStruct((m, n), out_dtype),
        scratch_shapes=[pltpu.VMEM((tm, d), BF16)],
        compiler_params=_params("parallel", "arbitrary"),
        name=name,
    )(x, g.reshape(1, d), w)


def _matmul_residual_kernel(a_ref, w_ref, x_ref, o_ref):
    o_ref[...] = x_ref[...] + jnp.dot(a_ref[...], w_ref[...], preferred_element_type=F32)


def matmul_residual(a, w, x, *, tm, tn, name):
    m, k = a.shape
    n = w.shape[1]
    return pl.pallas_call(
        _matmul_residual_kernel,
        grid=(m // tm, n // tn),
        in_specs=[
            pl.BlockSpec((tm, k), lambda i, j: (i, 0)),
            pl.BlockSpec((k, tn), lambda i, j: (0, j)),
            pl.BlockSpec((tm, tn), lambda i, j: (i, j)),
        ],
        out_specs=pl.BlockSpec((tm, tn), lambda i, j: (i, j)),
        out_shape=jax.ShapeDtypeStruct((m, n), F32),
        compiler_params=_params("parallel", "arbitrary"),
        name=name,
    )(a, w, x)


def _merge_kernel(oa_ref, or_ref, wa_ref, wr_ref, ga_ref, gr_ref, o_ref):
    ya = jnp.dot(oa_ref[...], wa_ref[...], preferred_element_type=F32)
    yr = jnp.dot(or_ref[...], wr_ref[...], preferred_element_type=F32)
    o_ref[...] = (jax.nn.sigmoid(ga_ref[...]) * ya + jax.nn.sigmoid(gr_ref[...]) * yr).astype(o_ref.dtype)


def merge_branches(o_attn, o_rnn, w_a, w_r, proj, ga_col, gr_col, *, tm, tn, name):
    m, k = o_attn.shape
    n = w_a.shape[1]
    ga_blk, gr_blk = ga_col // tn, gr_col // tn
    return pl.pallas_call(
        _merge_kernel,
        grid=(m // tm, n // tn),
        in_specs=[
            pl.BlockSpec((tm, k), lambda i, j: (i, 0)),
            pl.BlockSpec((tm, k), lambda i, j: (i, 0)),
            pl.BlockSpec((k, tn), lambda i, j: (0, j)),
            pl.BlockSpec((k, tn), lambda i, j: (0, j)),
            pl.BlockSpec((tm, tn), lambda i, j: (i, ga_blk + j)),
            pl.BlockSpec((tm, tn), lambda i, j: (i, gr_blk + j)),
        ],
        out_specs=pl.BlockSpec((tm, tn), lambda i, j: (i, j)),
        out_shape=jax.ShapeDtypeStruct((m, n), BF16),
        compiler_params=_params("parallel", "arbitrary"),
        name=name,
    )(o_attn, o_rnn, w_a, w_r, proj, proj)


def _rope_kernel(x_ref, cos_ref, sin_ref, gq_ref, gk_ref, o_ref, *, n_q_blocks, q_scale):
    is_q = pl.program_id(1) < n_q_blocks
    g = jnp.where(is_q, gq_ref[...] * q_scale, gk_ref[...])
    cos = cos_ref[...]
    sin = sin_ref[...]
    lane = lax.broadcasted_iota(jnp.int32, cos.shape, 1)
    first_half = (lane % (HEAD_DIM // 2)) < (HEAD_DIM // 4)
    for h in range(x_ref.shape[1] // HEAD_DIM):
        sl = slice(h * HEAD_DIM, (h + 1) * HEAD_DIM)
        x = x_ref[:, sl]
        xn = x * _rms_scale(x) * g
        swapped = jnp.where(first_half,
                            pltpu.roll(xn, HEAD_DIM - HEAD_DIM // 4, axis=1),
                            pltpu.roll(xn, HEAD_DIM // 4, axis=1))
        o_ref[:, sl] = (xn * cos + swapped * sin).astype(o_ref.dtype)


def rope_qk(proj, cos_t, sin_t, gq, gk, *, seq, n_qk_cols, n_q_cols, tm, tn, name):
    m = proj.shape[0]
    t_blocks = seq // tm
    return pl.pallas_call(
        functools.partial(_rope_kernel, n_q_blocks=n_q_cols // tn,
                          q_scale=HEAD_DIM ** -0.5 * LOG2E),
        grid=(m // tm, n_qk_cols // tn),
        in_specs=[
            pl.BlockSpec((tm, tn), lambda i, j: (i, j)),
            pl.BlockSpec((tm, HEAD_DIM), lambda i, j: (i % t_blocks, 0)),
            pl.BlockSpec((tm, HEAD_DIM), lambda i, j: (i % t_blocks, 0)),
            pl.BlockSpec((1, HEAD_DIM), lambda i, j: (0, 0)),
            pl.BlockSpec((1, HEAD_DIM), lambda i, j: (0, 0)),
        ],
        out_specs=pl.BlockSpec((tm, tn), lambda i, j: (i, j)),
        out_shape=jax.ShapeDtypeStruct((m, n_qk_cols), BF16),
        compiler_params=_params("parallel", "parallel"),
        name=name,
    )(proj, cos_t, sin_t, gq.reshape(1, HEAD_DIM), gk.reshape(1, HEAD_DIM))


def rope_tables(seq):
    rows_n = seq // GRID_W
    row = jnp.repeat(jnp.arange(rows_n, dtype=F32), GRID_W)
    col = jnp.tile(jnp.arange(GRID_W, dtype=F32), rows_n)
    n_freq = HEAD_DIM // 4
    inv = ROPE_THETA ** (-jnp.arange(n_freq, dtype=F32) / n_freq)
    ang_r = row[:, None] * inv[None, :]
    ang_c = col[:, None] * inv[None, :]
    cos_t = jnp.concatenate([jnp.cos(ang_r), jnp.cos(ang_r), jnp.cos(ang_c), jnp.cos(ang_c)], axis=-1)
    sin_t = jnp.concatenate([-jnp.sin(ang_r), jnp.sin(ang_r), -jnp.sin(ang_c), jnp.sin(ang_c)], axis=-1)
    return cos_t, sin_t


def _qk_phase(q_ref, k_ref, s_ref, m_ref, t, *, tq, kc, racc):
    seq = k_ref.shape[0]
    n = KV_GROUP * tq
    rows_q = pl.ds(pl.multiple_of(t * tq, tq), tq)
    q = jnp.concatenate([q_ref[rows_q, g * HEAD_DIM:(g + 1) * HEAD_DIM] for g in range(KV_GROUP)], axis=0)
    m_acc = jnp.full((racc, n), -jnp.inf, F32)
    for c in range(seq // kc):
        st = lax.dot_general(k_ref[c * kc:(c + 1) * kc, :], q, (((1,), (1,)), ((), ())),
                             preferred_element_type=F32)
        s_ref[c * kc:(c + 1) * kc, :] = st
        for r in range(kc // racc):
            m_acc = jnp.maximum(m_acc, st[r * racc:(r + 1) * racc, :])
    m_ref[...] = jnp.broadcast_to(jnp.max(m_acc, axis=0, keepdims=True), m_ref.shape)


def _pv_phase(s_ref, m_ref, vt_ref, o_ref, t, *, tq, kc, racc):
    seq = s_ref.shape[0]
    n = KV_GROUP * tq
    rows_q = pl.ds(pl.multiple_of(t * tq, tq), tq)
    m = m_ref[0:1, :]
    l_acc = jnp.zeros((racc, n), F32)
    acc = jnp.zeros((HEAD_DIM, n), F32)
    for c in range(seq // kc):
        p = jnp.exp2(s_ref[c * kc:(c + 1) * kc, :] - m)
        for r in range(kc // racc):
            l_acc = l_acc + p[r * racc:(r + 1) * racc, :]
        acc = acc + jnp.dot(vt_ref[:, c * kc:(c + 1) * kc], p.astype(BF16), preferred_element_type=F32)
    out = acc * (1.0 / jnp.sum(l_acc, axis=0, keepdims=True))
    for g in range(KV_GROUP):
        o_ref[rows_q, g * HEAD_DIM:(g + 1) * HEAD_DIM] = out[:, g * tq:(g + 1) * tq].T.astype(o_ref.dtype)


def _attn_kernel(q_ref, k_ref, v_ref, o_ref, s0_ref, s1_ref, m0_ref, m1_ref, vt_ref, *, tq, kc, racc):
    seq = k_ref.shape[0]
    nq = seq // tq
    for c in range(seq // kc):
        vt_ref[:, c * kc:(c + 1) * kc] = v_ref[c * kc:(c + 1) * kc, :].T.astype(BF16)
    qk = functools.partial(_qk_phase, q_ref, k_ref, tq=tq, kc=kc, racc=racc)
    pv = functools.partial(_pv_phase, vt_ref=vt_ref, o_ref=o_ref, tq=tq, kc=kc, racc=racc)
    qk(s0_ref, m0_ref, 0)

    def pair(i, carry):
        t = 2 * i
        qk(s1_ref, m1_ref, t + 1)
        pv(s0_ref, m0_ref, t=t)
        qk(s0_ref, m0_ref, t + 2)
        pv(s1_ref, m1_ref, t=t + 1)
        return carry

    lax.fori_loop(0, nq // 2 - 1, pair, 0)
    qk(s1_ref, m1_ref, nq - 1)
    pv(s0_ref, m0_ref, t=nq - 2)
    pv(s1_ref, m1_ref, t=nq - 1)


def gqa_attention(qk, proj, *, batch, seq, n_q_cols, v_col, tq, kc, racc, name):
    n_kv = n_q_cols // (KV_GROUP * HEAD_DIM)
    group_w = KV_GROUP * HEAD_DIM
    qk3 = qk.reshape(batch, seq, qk.shape[1])
    proj3 = proj.reshape(batch, seq, proj.shape[1])
    k_blk0 = n_q_cols // HEAD_DIM
    v_blk0 = v_col // HEAD_DIM
    n = KV_GROUP * tq
    assert (seq // tq) % 2 == 0 and seq % kc == 0 and kc % racc == 0
    out = pl.pallas_call(
        functools.partial(_attn_kernel, tq=tq, kc=kc, racc=racc),
        grid=(batch, n_kv),
        in_specs=[
            pl.BlockSpec((None, seq, group_w), lambda b, j: (b, 0, j)),
            pl.BlockSpec((None, seq, HEAD_DIM), lambda b, j: (b, 0, k_blk0 + j)),
            pl.BlockSpec((None, seq, HEAD_DIM), lambda b, j: (b, 0, v_blk0 + j)),
        ],
        out_specs=pl.BlockSpec((None, seq, group_w), lambda b, j: (b, 0, j)),
        out_shape=jax.ShapeDtypeStruct((batch, seq, n_q_cols), BF16),
        scratch_shapes=[pltpu.VMEM((seq, n), F32), pltpu.VMEM((seq, n), F32),
                        pltpu.VMEM((SUBLANES, n), F32), pltpu.VMEM((SUBLANES, n), F32),
                        pltpu.VMEM((HEAD_DIM, seq), BF16)],
        compiler_params=_params("parallel", "parallel"),
        name=name,
    )(qk3, qk3, proj3)
    return out.reshape(batch * seq, n_q_cols)


def _softplus(x):
    e = jnp.exp(-jnp.abs(x))
    u = 1.0 + e
    log1p_e = jnp.where(u == 1.0, e, jnp.log(u) * e / (u - 1.0))
    return jnp.maximum(x, 0.0) + log1p_e


def _scan8(a, b, reverse):
    row = lax.broadcasted_iota(jnp.int32, a.shape, 0)
    for d in (1, 2, 4):
        if reverse:
            valid = row < SUBLANES - d
            shift = SUBLANES - d
        else:
            valid = row >= d
            shift = d
        a_prev = jnp.where(valid, pltpu.roll(a, shift, axis=0), 1.0)
        b_prev = jnp.where(valid, pltpu.roll(b, shift, axis=0), 0.0)
        b = a * b_prev + b
        a = a * a_prev
    return a, b


SQRT_TINY = 1e-30
PITCH_PAD = SUBLANES


def _shift_rows(x, down):
    row = lax.broadcasted_iota(jnp.int32, x.shape, 0)
    if down:
        return jnp.where(row >= 1, pltpu.roll(x, 1, axis=0), 0.0)
    return jnp.where(row < SUBLANES - 1, pltpu.roll(x, SUBLANES - 1, axis=0), 0.0)


def _lru_kernel(u_ref, y_ref, cw_ref, cb_ref, wg_ref, bg_ref, lam_ref, o_ref,
                un_ref, ui_ref, af_ref, bf_ref, ab_ref, bb_ref, pf_ref, hf_ref, pb_ref, hb_ref, hn_ref,
                *, tc, unroll):
    seq, cb = u_ref.shape
    nt = seq // SUBLANES
    pitch = nt + PITCH_PAD
    halo = CONV_LEFT

    for s in range(SUBLANES):
        un_ref[s * pitch:s * pitch + nt, :] = u_ref[s * nt:(s + 1) * nt, :]

    def gather(t, carry):
        ui_ref[pl.ds(pl.multiple_of((t + halo) * SUBLANES, SUBLANES), SUBLANES), :] = \
            un_ref[pl.ds(t, SUBLANES, stride=pitch), :]
        return carry

    lax.fori_loop(0, nt, gather, 0, unroll=unroll)
    for k in range(halo):
        src = (nt + k) * SUBLANES
        ui_ref[k * SUBLANES:(k + 1) * SUBLANES, :] = _shift_rows(ui_ref[src:src + SUBLANES, :], down=True)
    for k in range(CONV_W - 1 - CONV_LEFT):
        src = (halo + k) * SUBLANES
        dst = (halo + nt + k) * SUBLANES
        ui_ref[dst:dst + SUBLANES, :] = _shift_rows(ui_ref[src:src + SUBLANES, :], down=False)

    wg = wg_ref[...] * 0.5
    bg = bg_ref[...] * 0.5
    half_decay = [(-0.5 * LRU_C * LOG2E) * _softplus(-lam_ref[d:d + 1, :]) for d in range(2)]

    for c in range(seq // tc):
        r0 = c * tc
        uf = jnp.broadcast_to(cb_ref[...], (tc, cb))
        for tap in range(CONV_W):
            uf = uf + ui_ref[r0 + tap * SUBLANES:r0 + tap * SUBLANES + tc, :] * cw_ref[tap:tap + 1, :]
        gates = jnp.dot(uf.astype(BF16), wg, preferred_element_type=F32) + bg
        uh = 0.5 * uf
        for d, (a_ref, b_ref) in enumerate(((af_ref, bf_ref), (ab_ref, bb_ref))):
            tr = jnp.tanh(gates[:, (2 * d) * cb:(2 * d + 1) * cb])
            ti = jnp.tanh(gates[:, (2 * d + 1) * cb:(2 * d + 2) * cb])
            a = jnp.exp2(half_decay[d] * tr + half_decay[d])
            iu = uh * ti + uh
            om = 1.0 - a * a
            a_ref[r0:r0 + tc, :] = a
            b_ref[r0:r0 + tc, :] = (om * lax.rsqrt(jnp.maximum(om, SQRT_TINY))) * iu

    def scan_body(t, carry):
        hf, pf, hb, pb = carry
        rf = pl.ds(pl.multiple_of(t * SUBLANES, SUBLANES), SUBLANES)
        rb = pl.ds(pl.multiple_of((nt - 1 - t) * SUBLANES, SUBLANES), SUBLANES)
        a = af_ref[rf, :]
        hf = a * hf + bf_ref[rf, :]
        pf = a * pf
        hf_ref[rf, :] = hf
        pf_ref[rf, :] = pf
        a = ab_ref[rb, :]
        hb = a * hb + bb_ref[rb, :]
        pb = a * pb
        hb_ref[rb, :] = hb
        pb_ref[rb, :] = pb
        return hf, pf, hb, pb

    zero = jnp.zeros((SUBLANES, cb), F32)
    one = jnp.ones((SUBLANES, cb), F32)
    hf, pf, hb, pb = lax.fori_loop(0, nt, scan_body, (zero, one, zero, one), unroll=unroll)
    init_f = _shift_rows(_scan8(pf, hf, reverse=False)[1], down=True)
    init_b = _shift_rows(_scan8(pb, hb, reverse=True)[1], down=False)

    def fix(t, carry):
        r = pl.ds(pl.multiple_of(t * SUBLANES, SUBLANES), SUBLANES)
        h = (hf_ref[r, :] + pf_ref[r, :] * init_f) + (hb_ref[r, :] + pb_ref[r, :] * init_b)
        hn_ref[pl.ds(t, SUBLANES, stride=pitch), :] = h
        return carry

    lax.fori_loop(0, nt, fix, 0, unroll=unroll)

    for s in range(SUBLANES):
        for c in range(nt // tc):
            src = slice(s * pitch + c * tc, s * pitch + (c + 1) * tc)
            dst = slice(s * nt + c * tc, s * nt + (c + 1) * tc)
            o_ref[dst, :] = (hn_ref[src, :] * jax.nn.gelu(y_ref[dst, :])).astype(o_ref.dtype)


def rglru_branch(proj, conv_w, conv_b, w_gates, b_gates, lam, *, batch, seq, u_col, y_col, d_rnn, name):
    cb = LANES
    n_blk = d_rnn // cb
    proj3 = proj.reshape(batch, seq, proj.shape[1])
    u_blk0, y_blk0 = u_col // cb, y_col // cb
    nt = seq // SUBLANES
    padded = SUBLANES * (nt + PITCH_PAD)
    tc = 256
    assert nt % tc == 0
    out = pl.pallas_call(
        functools.partial(_lru_kernel, tc=tc, unroll=8),
        grid=(batch, n_blk),
        in_specs=[
            pl.BlockSpec((None, seq, cb), lambda b, n: (b, 0, u_blk0 + n)),
            pl.BlockSpec((None, seq, cb), lambda b, n: (b, 0, y_blk0 + n)),
            pl.BlockSpec((CONV_W, cb), lambda b, n: (0, n)),
            pl.BlockSpec((1, cb), lambda b, n: (0, n)),
            pl.BlockSpec((None, cb, 4 * cb), lambda b, n: (n, 0, 0)),
            pl.BlockSpec((None, 1, 4 * cb), lambda b, n: (n, 0, 0)),
            pl.BlockSpec((2, cb), lambda b, n: (0, n)),
        ],
        out_specs=pl.BlockSpec((None, seq, cb), lambda b, n: (b, 0, n)),
        out_shape=jax.ShapeDtypeStruct((batch, seq, d_rnn), BF16),
        scratch_shapes=[pltpu.VMEM((padded, cb), F32),
                        pltpu.VMEM((seq + (CONV_W - 1) * SUBLANES, cb), F32)]
                       + [pltpu.VMEM((seq, cb), F32)] * 8 + [pltpu.VMEM((padded, cb), F32)],
        compiler_params=_params("parallel", "parallel"),
        name=name,
    )(proj3, proj3, conv_w, conv_b.reshape(1, d_rnn), w_gates, b_gates, lam)
    return out.reshape(batch * seq, d_rnn)


def _xattn_kernel(q_ref, kv_ref, o_ref, *, n_heads):
    d_model = q_ref.shape[1]
    hd = d_model // n_heads
    scale = hd ** -0.5 * LOG2E
    for h in range(n_heads):
        q = q_ref[:, h * hd:(h + 1) * hd]
        k = kv_ref[:, h * hd:(h + 1) * hd]
        v = kv_ref[:, d_model + h * hd:d_model + (h + 1) * hd]
        s = lax.dot_general(q, k, (((1,), (1,)), ((), ())), preferred_element_type=F32) * scale
        p = jnp.exp2(s - jnp.max(s, axis=-1, keepdims=True))
        o = jnp.dot(p.astype(BF16), v, preferred_element_type=F32) / jnp.sum(p, axis=-1, keepdims=True)
        o_ref[:, h * hd:(h + 1) * hd] = o.astype(o_ref.dtype)


def cross_attention(xq, kv, *, batch, seq, tq, name):
    d_model = xq.shape[1]
    n_mem = kv.shape[0] // batch
    out = pl.pallas_call(
        functools.partial(_xattn_kernel, n_heads=N_XHEADS),
        grid=(batch, seq // tq),
        in_specs=[
            pl.BlockSpec((None, tq, d_model), lambda b, i: (b, i, 0)),
            pl.BlockSpec((None, n_mem, 2 * d_model), lambda b, i: (b, 0, 0)),
        ],
        out_specs=pl.BlockSpec((None, tq, d_model), lambda b, i: (b, i, 0)),
        out_shape=jax.ShapeDtypeStruct((batch, seq, d_model), BF16),
        compiler_params=_params("parallel", "parallel"),
        name=name,
    )(xq.reshape(batch, seq, d_model), kv.reshape(batch, n_mem, 2 * d_model))
    return out.reshape(batch * seq, d_model)


def _rmsnorm_kernel(x_ref, g_ref, o_ref):
    x = x_ref[...]
    o_ref[...] = x * _rms_scale(x) * g_ref[...]


def rmsnorm(x, g, *, tm, name):
    m, d = x.shape
    return pl.pallas_call(
        _rmsnorm_kernel,
        grid=(m // tm,),
        in_specs=[pl.BlockSpec((tm, d), lambda i: (i, 0)), pl.BlockSpec((1, d), lambda i: (0, 0))],
        out_specs=pl.BlockSpec((tm, d), lambda i: (i, 0)),
        out_shape=jax.ShapeDtypeStruct((m, d), F32),
        compiler_params=_params("parallel"),
        name=name,
    )(x, g.reshape(1, d))


def kernel(x, mem, mix_norm_g, w_in, q_norm_g, k_norm_g, conv_w, conv_b, lru_w_r, lru_b_r, lru_w_i, lru_b_i, lru_lambda, w_attn_branch, w_rnn_branch, w_mix_out, cross_norm_g, mem_norm_g, w_xq, w_xkv, w_xo, mlp_norm_g, w_up, w_down, final_norm_g):
    batch, seq, d_model = x.shape
    n_mem = mem.shape[1]
    depth = w_in.shape[0]
    attn_w = w_attn_branch.shape[1]
    d_rnn = w_rnn_branch.shape[1]
    kv_w = attn_w // KV_GROUP
    n_blocks, rnn_block = lru_w_r.shape[2], lru_w_r.shape[3]
    k_col = attn_w
    v_col = k_col + kv_w
    u_col = v_col + kv_w
    y_col = u_col + d_rnn
    ga_col = y_col + d_rnn
    gr_col = ga_col + d_model
    assert w_in.shape[2] == gr_col + d_model and rnn_block == LANES

    xf = x.reshape(batch * seq, d_model)
    memf = mem.reshape(batch * n_mem, d_model)
    cos_t, sin_t = rope_tables(seq)

    for l in range(depth):
        tag = f"l{l}_"
        proj = norm_matmul(xf, mix_norm_g[l], w_in[l].astype(BF16), out_dtype=F32, tm=1024, tn=1024,
                           name=tag + "in_proj")
        qk = rope_qk(proj, cos_t, sin_t, q_norm_g[l], k_norm_g[l], seq=seq, n_qk_cols=v_col,
                     n_q_cols=attn_w, tm=512, tn=512, name=tag + "rope")
        o_attn = gqa_attention(qk, proj, batch=batch, seq=seq, n_q_cols=attn_w, v_col=v_col,
                               tq=128, kc=512, racc=32, name=tag + "attn")
        w_gates = jnp.concatenate([lru_w_r[l, 0], lru_w_i[l, 0], lru_w_r[l, 1], lru_w_i[l, 1]],
                                  axis=-1).astype(BF16)
        b_gates = jnp.concatenate(
            [b.reshape(n_blocks, 1, rnn_block) for b in (lru_b_r[l, 0], lru_b_i[l, 0], lru_b_r[l, 1], lru_b_i[l, 1])],
            axis=-1)
        o_rnn = rglru_branch(proj, conv_w[l], conv_b[l], w_gates, b_gates, lru_lambda[l], batch=batch, seq=seq,
                             u_col=u_col, y_col=y_col, d_rnn=d_rnn, name=tag + "rglru")
        merged = merge_branches(o_attn, o_rnn, w_attn_branch[l].astype(BF16), w_rnn_branch[l].astype(BF16),
                                proj, ga_col, gr_col, tm=1024, tn=512, name=tag + "merge")
        xf = matmul_residual(merged, w_mix_out[l].astype(BF16), xf, tm=1024, tn=1024, name=tag + "mix_out")
        xq = norm_matmul(xf, cross_norm_g[l], w_xq[l].astype(BF16), out_dtype=BF16, tm=1024, tn=1024,
                         name=tag + "xq")
        kv = norm_matmul(memf, mem_norm_g[l], w_xkv[l].astype(BF16), out_dtype=BF16, tm=batch * n_mem, tn=1024,
                         name=tag + "xkv")
        xo = cross_attention(xq, kv, batch=batch, seq=seq, tq=512, name=tag + "xattn")
        xf = matmul_residual(xo, w_xo[l].astype(BF16), xf, tm=1024, tn=1024, name=tag + "xo")
        hidden = norm_matmul(xf, mlp_norm_g[l], w_up[l].astype(BF16), out_dtype=BF16, tm=1024, tn=1024,
                             relu2=True, name=tag + "mlp_up")
        xf = matmul_residual(hidden, w_down[l].astype(BF16), xf, tm=512, tn=512, name=tag + "mlp_down")
    out = rmsnorm(xf, final_norm_g, tm=512, name="final_norm")
    return out.reshape(batch, seq, d_model)
```

```python
import functools
import math

import jax
import jax.numpy as jnp
from jax import lax
from jax.experimental import pallas as pl
from jax.experimental.pallas import tpu as pltpu

F32 = jnp.float32
BF16 = jnp.bfloat16

GRID_W = 64
HEAD_DIM = 128
KV_GROUP = 4
ROPE_THETA = 10000.0
CONV_W = 4
CONV_LEFT = CONV_W // 2
LRU_C = 8.0
N_XHEADS = 4
EPS = 1e-6
LOG2E = math.log2(math.e)

LANES = 128
SUBLANES = 8
VMEM_LIMIT_BYTES = 56 * 1024 * 1024


def _params(*sem):
    return pltpu.CompilerParams(dimension_semantics=sem, vmem_limit_bytes=VMEM_LIMIT_BYTES)


def _resident(shape):
    return pl.BlockSpec(shape, lambda *_: (0,) * len(shape), pipeline_mode=pl.Buffered(1))


def _rms_scale(x):
    return lax.rsqrt(jnp.mean(x * x, axis=-1, keepdims=True) + EPS)


def _sigmoid(x):
    return 0.5 * jnp.tanh(0.5 * x) + 0.5


def _norm_matmul_kernel(x_ref, g_ref, w_ref, o_ref, h_ref):
    @pl.when(pl.program_id(1) == 0)
    def _():
        x = x_ref[...]
        h_ref[...] = (x * _rms_scale(x) * g_ref[...]).astype(BF16)

    o_ref[...] = jnp.dot(h_ref[...], w_ref[...], preferred_element_type=F32).astype(o_ref.dtype)


def norm_matmul(x, g, w, *, out_dtype, tm, tn, name):
    m, d = x.shape
    n = w.shape[1]
    return pl.pallas_call(
        _norm_matmul_kernel,
        grid=(m // tm, n // tn),
        in_specs=[
            pl.BlockSpec((tm, d), lambda i, j: (i, 0)),
            pl.BlockSpec((1, d), lambda i, j: (0, 0)),
            pl.BlockSpec((d, tn), lambda i, j: (0, j)),
        ],
        out_specs=pl.BlockSpec((tm, tn), lambda i, j: (i, j)),
        out_shape=jax.ShapeDtypeStruct((m, n), out_dtype),
        scratch_shapes=[pltpu.VMEM((tm, d), BF16)],
        compiler_params=_params("parallel", "arbitrary"),
        name=name,
    )(x, g.reshape(1, d), w)


def _in_proj_kernel(x_ref, g_ref, w_ref, cos_ref, sin_ref, gq_ref, gk_ref,
                    qk_ref, v_ref, u_ref, gt_ref, h_ref, *, n_q, n_k, n_v, n_u, q_scale):
    j = pl.program_id(1)
    c_v = n_q + n_k
    c_u = c_v + n_v
    c_g = c_u + n_u

    @pl.when(j == 0)
    def _():
        x = x_ref[...]
        h_ref[...] = (x * _rms_scale(x) * g_ref[...]).astype(BF16)

    y = jnp.dot(h_ref[...], w_ref[...], preferred_element_type=F32)

    @pl.when(j < c_v)
    def _():
        g = jnp.where(j < n_q, gq_ref[...] * q_scale, gk_ref[...])
        cos = cos_ref[...]
        sin = sin_ref[...]
        lane = lax.broadcasted_iota(jnp.int32, cos.shape, 1)
        first_half = (lane % (HEAD_DIM // 2)) < (HEAD_DIM // 4)
        for h in range(y.shape[1] // HEAD_DIM):
            sl = slice(h * HEAD_DIM, (h + 1) * HEAD_DIM)
            xh = y[:, sl]
            xn = xh * _rms_scale(xh) * g
            swapped = jnp.where(first_half,
                                pltpu.roll(xn, HEAD_DIM - HEAD_DIM // 4, axis=1),
                                pltpu.roll(xn, HEAD_DIM // 4, axis=1))
            qk_ref[:, sl] = (xn * cos + swapped * sin).astype(qk_ref.dtype)

    @pl.when((j >= c_v) & (j < c_u))
    def _():
        v_ref[...] = y.astype(v_ref.dtype)

    @pl.when((j >= c_u) & (j < c_g))
    def _():
        u_ref[...] = y

    @pl.when(j >= c_g)
    def _():
        gt_ref[...] = y.astype(gt_ref.dtype)


def in_proj(x, g, w, cos_t, sin_t, gq, gk, *, seq, attn_w, kv_w, d_rnn, tm, tn, name):
    m, d = x.shape
    n = w.shape[1]
    n_q, n_k, n_v, n_u = attn_w // tn, kv_w // tn, kv_w // tn, d_rnn // tn
    c_v = n_q + n_k
    c_u = c_v + n_v
    c_g = c_u + n_u
    n_g = n // tn - c_g
    assert min(n_q, n_k, n_v, n_u, n_g) >= 1 and (attn_w + 2 * kv_w + d_rnn) % tn == 0 and seq % tm == 0
    t_blocks = seq // tm
    return pl.pallas_call(
        functools.partial(_in_proj_kernel, n_q=n_q, n_k=n_k, n_v=n_v, n_u=n_u,
                          q_scale=HEAD_DIM ** -0.5 * LOG2E),
        grid=(m // tm, n // tn),
        in_specs=[
            pl.BlockSpec((tm, d), lambda i, j: (i, 0)),
            pl.BlockSpec((1, d), lambda i, j: (0, 0)),
            pl.BlockSpec((d, tn), lambda i, j: (0, j)),
            pl.BlockSpec((tm, HEAD_DIM), lambda i, j: (i % t_blocks, 0)),
            pl.BlockSpec((tm, HEAD_DIM), lambda i, j: (i % t_blocks, 0)),
            pl.BlockSpec((1, HEAD_DIM), lambda i, j: (0, 0)),
            pl.BlockSpec((1, HEAD_DIM), lambda i, j: (0, 0)),
        ],
        out_specs=[
            pl.BlockSpec((tm, tn), lambda i, j: (i, jnp.minimum(j, c_v - 1))),
            pl.BlockSpec((tm, tn), lambda i, j: (i, jnp.clip(j - c_v, 0, n_v - 1))),
            pl.BlockSpec((tm, tn), lambda i, j: (i, jnp.clip(j - c_u, 0, n_u - 1))),
            pl.BlockSpec((tm, tn), lambda i, j: (i, jnp.maximum(j - c_g, 0))),
        ],
        out_shape=[
            jax.ShapeDtypeStruct((m, attn_w + kv_w), BF16),
            jax.ShapeDtypeStruct((m, kv_w), BF16),
            jax.ShapeDtypeStruct((m, d_rnn), F32),
            jax.ShapeDtypeStruct((m, n_g * tn), BF16),
        ],
        scratch_shapes=[pltpu.VMEM((tm, d), BF16)],
        compiler_params=_params("parallel", "arbitrary"),
        name=name,
    )(x, g.reshape(1, d), w, cos_t, sin_t, gq.reshape(1, HEAD_DIM), gk.reshape(1, HEAD_DIM))


def rope_tables(seq):
    rows_n = seq // GRID_W
    row = jnp.repeat(jnp.arange(rows_n, dtype=F32), GRID_W)
    col = jnp.tile(jnp.arange(GRID_W, dtype=F32), rows_n)
    n_freq = HEAD_DIM // 4
    inv = ROPE_THETA ** (-jnp.arange(n_freq, dtype=F32) / n_freq)
    ang_r = row[:, None] * inv[None, :]
    ang_c = col[:, None] * inv[None, :]
    cos_t = jnp.concatenate([jnp.cos(ang_r), jnp.cos(ang_r), jnp.cos(ang_c), jnp.cos(ang_c)], axis=-1)
    sin_t = jnp.concatenate([-jnp.sin(ang_r), jnp.sin(ang_r), -jnp.sin(ang_c), jnp.sin(ang_c)], axis=-1)
    return cos_t, sin_t


def _qk_phase(q_ref, k_ref, s_ref, m_ref, t, *, tq, kc, racc):
    seq = k_ref.shape[0]
    n = KV_GROUP * tq
    rows_q = pl.ds(pl.multiple_of(t * tq, tq), tq)
    q = jnp.concatenate([q_ref[rows_q, g * HEAD_DIM:(g + 1) * HEAD_DIM] for g in range(KV_GROUP)], axis=0)
    m_acc = jnp.full((racc, n), -jnp.inf, F32)
    for c in range(seq // kc):
        st = lax.dot_general(k_ref[c * kc:(c + 1) * kc, :], q, (((1,), (1,)), ((), ())),
                             preferred_element_type=F32)
        s_ref[c * kc:(c + 1) * kc, :] = st
        for r in range(kc // racc):
            m_acc = jnp.maximum(m_acc, st[r * racc:(r + 1) * racc, :])
    m_ref[...] = jnp.broadcast_to(jnp.max(m_acc, axis=0, keepdims=True), m_ref.shape)


def _pv_phase(s_ref, m_ref, vt_ref, o_ref, t, *, tq, kc, racc):
    seq = s_ref.shape[0]
    n = KV_GROUP * tq
    rows_q = pl.ds(pl.multiple_of(t * tq, tq), tq)
    m = m_ref[0:1, :]
    l_acc = jnp.zeros((racc, n), F32)
    acc = jnp.zeros((HEAD_DIM, n), F32)
    for c in range(seq // kc):
        p = jnp.exp2(s_ref[c * kc:(c + 1) * kc, :] - m)
        for r in range(kc // racc):
            l_acc = l_acc + p[r * racc:(r + 1) * racc, :]
        acc = acc + jnp.dot(vt_ref[:, c * kc:(c + 1) * kc], p.astype(BF16), preferred_element_type=F32)
    out = acc * (1.0 / jnp.sum(l_acc, axis=0, keepdims=True))
    for g in range(KV_GROUP):
        o_ref[rows_q, g * HEAD_DIM:(g + 1) * HEAD_DIM] = out[:, g * tq:(g + 1) * tq].T.astype(o_ref.dtype)


def _attn_kernel(q_ref, k_ref, v_ref, o_ref, s0_ref, s1_ref, m0_ref, m1_ref, vt_ref, *, tq, kc, racc):
    seq = k_ref.shape[0]
    nq = seq // tq
    for c in range(seq // kc):
        vt_ref[:, c * kc:(c + 1) * kc] = v_ref[c * kc:(c + 1) * kc, :].astype(F32).T.astype(BF16)
    qk = functools.partial(_qk_phase, q_ref, k_ref, tq=tq, kc=kc, racc=racc)
    pv = functools.partial(_pv_phase, vt_ref=vt_ref, o_ref=o_ref, tq=tq, kc=kc, racc=racc)
    qk(s0_ref, m0_ref, 0)

    def pair(i, carry):
        t = 2 * i
        qk(s1_ref, m1_ref, t + 1)
        pv(s0_ref, m0_ref, t=t)
        qk(s0_ref, m0_ref, t + 2)
        pv(s1_ref, m1_ref, t=t + 1)
        return carry

    lax.fori_loop(0, nq // 2 - 1, pair, 0)
    qk(s1_ref, m1_ref, nq - 1)
    pv(s0_ref, m0_ref, t=nq - 2)
    pv(s1_ref, m1_ref, t=nq - 1)


def gqa_attention(qk, v, *, batch, seq, n_q_cols, tq, kc, racc, name):
    n_kv = n_q_cols // (KV_GROUP * HEAD_DIM)
    group_w = KV_GROUP * HEAD_DIM
    qk3 = qk.reshape(batch, seq, qk.shape[1])
    v3 = v.reshape(batch, seq, v.shape[1])
    k_blk0 = n_q_cols // HEAD_DIM
    n = KV_GROUP * tq
    assert (seq // tq) % 2 == 0 and seq % kc == 0 and kc % racc == 0
    out = pl.pallas_call(
        functools.partial(_attn_kernel, tq=tq, kc=kc, racc=racc),
        grid=(batch, n_kv),
        in_specs=[
            pl.BlockSpec((None, seq, group_w), lambda b, j: (b, 0, j)),
            pl.BlockSpec((None, seq, HEAD_DIM), lambda b, j: (b, 0, k_blk0 + j)),
            pl.BlockSpec((None, seq, HEAD_DIM), lambda b, j: (b, 0, j)),
        ],
        out_specs=pl.BlockSpec((None, seq, group_w), lambda b, j: (b, 0, j)),
        out_shape=jax.ShapeDtypeStruct((batch, seq, n_q_cols), BF16),
        scratch_shapes=[pltpu.VMEM((seq, n), F32), pltpu.VMEM((seq, n), F32),
                        pltpu.VMEM((SUBLANES, n), F32), pltpu.VMEM((SUBLANES, n), F32),
                        pltpu.VMEM((HEAD_DIM, seq), BF16)],
        compiler_params=_params("parallel", "parallel"),
        name=name,
    )(qk3, qk3, v3)
    return out.reshape(batch * seq, n_q_cols)


def _softplus(x):
    e = jnp.exp(-jnp.abs(x))
    u = 1.0 + e
    log1p_e = jnp.where(u == 1.0, e, jnp.log(u) * e / (u - 1.0))
    return jnp.maximum(x, 0.0) + log1p_e


def _scan8(a, b, reverse):
    row = lax.broadcasted_iota(jnp.int32, a.shape, 0)
    for d in (1, 2, 4):
        if reverse:
            valid = row < SUBLANES - d
            shift = SUBLANES - d
        else:
            valid = row >= d
            shift = d
        a_prev = jnp.where(valid, pltpu.roll(a, shift, axis=0), 1.0)
        b_prev = jnp.where(valid, pltpu.roll(b, shift, axis=0), 0.0)
        b = a * b_prev + b
        a = a * a_prev
    return a, b


SQRT_TINY = 1e-30
PITCH_PAD = SUBLANES


def _shift_rows(x, down):
    row = lax.broadcasted_iota(jnp.int32, x.shape, 0)
    if down:
        return jnp.where(row >= 1, pltpu.roll(x, 1, axis=0), 0.0)
    return jnp.where(row < SUBLANES - 1, pltpu.roll(x, SUBLANES - 1, axis=0), 0.0)


def _lru_kernel(u_ref, y_ref, cw_ref, cb_ref, wg_ref, bg_ref, lam_ref, o_ref,
                un_ref, ui_ref, af_ref, bf_ref, ab_ref, bb_ref, pf_ref, hf_ref, pb_ref, hb_ref, hn_ref,
                *, tc, unroll):
    seq, cb = u_ref.shape
    nt = seq // SUBLANES
    pitch = nt + PITCH_PAD
    halo = CONV_LEFT

    for s in range(SUBLANES):
        un_ref[s * pitch:s * pitch + nt, :] = u_ref[s * nt:(s + 1) * nt, :]

    def gather(t, carry):
        ui_ref[pl.ds(pl.multiple_of((t + halo) * SUBLANES, SUBLANES), SUBLANES), :] = \
            un_ref[pl.ds(t, SUBLANES, stride=pitch), :]
        return carry

    lax.fori_loop(0, nt, gather, 0, unroll=unroll)
    for k in range(halo):
        src = (nt + k) * SUBLANES
        ui_ref[k * SUBLANES:(k + 1) * SUBLANES, :] = _shift_rows(ui_ref[src:src + SUBLANES, :], down=True)
    for k in range(CONV_W - 1 - CONV_LEFT):
        src = (halo + k) * SUBLANES
        dst = (halo + nt + k) * SUBLANES
        ui_ref[dst:dst + SUBLANES, :] = _shift_rows(ui_ref[src:src + SUBLANES, :], down=False)

    wg = wg_ref[...] * 0.5
    bg = bg_ref[...] * 0.5
    half_decay = [(-0.5 * LRU_C * LOG2E) * _softplus(-lam_ref[d:d + 1, :]) for d in range(2)]

    for c in range(seq // tc):
        r0 = c * tc
        uf = jnp.broadcast_to(cb_ref[...], (tc, cb))
        for tap in range(CONV_W):
            uf = uf + ui_ref[r0 + tap * SUBLANES:r0 + tap * SUBLANES + tc, :] * cw_ref[tap:tap + 1, :]
        gates = jnp.dot(uf.astype(BF16), wg, preferred_element_type=F32) + bg
        uh = 0.5 * uf
        for d, (a_ref, b_ref) in enumerate(((af_ref, bf_ref), (ab_ref, bb_ref))):
            tr = jnp.tanh(gates[:, (2 * d) * cb:(2 * d + 1) * cb])
            ti = jnp.tanh(gates[:, (2 * d + 1) * cb:(2 * d + 2) * cb])
            a = jnp.exp2(half_decay[d] * tr + half_decay[d])
            iu = uh * ti + uh
            om = 1.0 - a * a
            a_ref[r0:r0 + tc, :] = a
            b_ref[r0:r0 + tc, :] = (om * lax.rsqrt(jnp.maximum(om, SQRT_TINY))) * iu

    def scan_body(t, carry):
        hf, pf, hb, pb = carry
        rf = pl.ds(pl.multiple_of(t * SUBLANES, SUBLANES), SUBLANES)
        rb = pl.ds(pl.multiple_of((nt - 1 - t) * SUBLANES, SUBLANES), SUBLANES)
        a = af_ref[rf, :]
        hf = a * hf + bf_ref[rf, :]
        pf = a * pf
        hf_ref[rf, :] = hf
        pf_ref[rf, :] = pf
        a = ab_ref[rb, :]
        hb = a * hb + bb_ref[rb, :]
        pb = a * pb
        hb_ref[rb, :] = hb
        pb_ref[rb, :] = pb
        return hf, pf, hb, pb

    zero = jnp.zeros((SUBLANES, cb), F32)
    one = jnp.ones((SUBLANES, cb), F32)
    hf, pf, hb, pb = lax.fori_loop(0, nt, scan_body, (zero, one, zero, one), unroll=unroll)
    init_f = _shift_rows(_scan8(pf, hf, reverse=False)[1], down=True)
    init_b = _shift_rows(_scan8(pb, hb, reverse=True)[1], down=False)

    def fix(t, carry):
        r = pl.ds(pl.multiple_of(t * SUBLANES, SUBLANES), SUBLANES)
        h = (hf_ref[r, :] + pf_ref[r, :] * init_f) + (hb_ref[r, :] + pb_ref[r, :] * init_b)
        hn_ref[pl.ds(t, SUBLANES, stride=pitch), :] = h
        return carry

    lax.fori_loop(0, nt, fix, 0, unroll=unroll)

    for s in range(SUBLANES):
        for c in range(nt // tc):
            src = slice(s * pitch + c * tc, s * pitch + (c + 1) * tc)
            dst = slice(s * nt + c * tc, s * nt + (c + 1) * tc)
            o_ref[dst, :] = (hn_ref[src, :] * jax.nn.gelu(y_ref[dst, :].astype(F32))).astype(o_ref.dtype)


def rglru_branch(u, gates, conv_w, conv_b, w_gates, b_gates, lam, *, batch, seq, name):
    cb = LANES
    d_rnn = u.shape[1]
    n_blk = d_rnn // cb
    u3 = u.reshape(batch, seq, d_rnn)
    gates3 = gates.reshape(batch, seq, gates.shape[1])
    nt = seq // SUBLANES
    padded = SUBLANES * (nt + PITCH_PAD)
    tc = 256
    assert nt % tc == 0
    out = pl.pallas_call(
        functools.partial(_lru_kernel, tc=tc, unroll=8),
        grid=(batch, n_blk),
        in_specs=[
            pl.BlockSpec((None, seq, cb), lambda b, n: (b, 0, n)),
            pl.BlockSpec((None, seq, cb), lambda b, n: (b, 0, n)),
            pl.BlockSpec((CONV_W, cb), lambda b, n: (0, n)),
            pl.BlockSpec((1, cb), lambda b, n: (0, n)),
            pl.BlockSpec((None, cb, 4 * cb), lambda b, n: (n, 0, 0)),
            pl.BlockSpec((None, 1, 4 * cb), lambda b, n: (n, 0, 0)),
            pl.BlockSpec((2, cb), lambda b, n: (0, n)),
        ],
        out_specs=pl.BlockSpec((None, seq, cb), lambda b, n: (b, 0, n)),
        out_shape=jax.ShapeDtypeStruct((batch, seq, d_rnn), BF16),
        scratch_shapes=[pltpu.VMEM((padded, cb), F32),
                        pltpu.VMEM((seq + (CONV_W - 1) * SUBLANES, cb), F32)]
                       + [pltpu.VMEM((seq, cb), F32)] * 8 + [pltpu.VMEM((padded, cb), F32)],
        compiler_params=_params("parallel", "parallel"),
        name=name,
    )(u3, gates3, conv_w, conv_b.reshape(1, d_rnn), w_gates, b_gates, lam)
    return out.reshape(batch * seq, d_rnn)


def _merge_mix_kernel(oa_ref, or_ref, ga_ref, gr_ref, x_ref, wa_ref, wr_ref, wo_ref, o_ref, mg_ref, *, nc):
    d = wo_ref.shape[1]
    oa = oa_ref[...]
    orn = or_ref[...]
    for c in range(d // nc):
        sl = slice(c * nc, (c + 1) * nc)
        ya = jnp.dot(oa, wa_ref[:, sl], preferred_element_type=F32)
        yr = jnp.dot(orn, wr_ref[:, sl], preferred_element_type=F32)
        mg_ref[:, sl] = (_sigmoid(ga_ref[:, sl].astype(F32)) * ya
                         + _sigmoid(gr_ref[:, sl].astype(F32)) * yr).astype(BF16)
    mg = mg_ref[...]
    for c in range(d // nc):
        sl = slice(c * nc, (c + 1) * nc)
        o_ref[:, sl] = x_ref[:, sl] + jnp.dot(mg, wo_ref[:, sl], preferred_element_type=F32)


def merge_mix(o_attn, o_rnn, gates, x, w_a, w_r, w_o, *, ga_col, gr_col, tm, nc, name):
    m, d = x.shape
    assert ga_col % d == 0 and gr_col % d == 0 and w_a.shape == w_r.shape == w_o.shape == (d, d)
    row = lambda i: (i, 0)
    return pl.pallas_call(
        functools.partial(_merge_mix_kernel, nc=nc),
        grid=(m // tm,),
        in_specs=[
            pl.BlockSpec((tm, d), row),
            pl.BlockSpec((tm, d), row),
            pl.BlockSpec((tm, d), lambda i: (i, ga_col // d)),
            pl.BlockSpec((tm, d), lambda i: (i, gr_col // d)),
            pl.BlockSpec((tm, d), row),
            _resident((d, d)), _resident((d, d)), _resident((d, d)),
        ],
        out_specs=pl.BlockSpec((tm, d), row),
        out_shape=jax.ShapeDtypeStruct((m, d), F32),
        scratch_shapes=[pltpu.VMEM((tm, d), BF16)],
        compiler_params=_params("parallel"),
        name=name,
    )(o_attn, o_rnn, gates, gates, x, w_a, w_r, w_o)


def _cross_kernel(x_ref, g_ref, kv_ref, wq_ref, wo_ref, o_ref, q_ref, a_ref, *, n_heads, nc):
    d = x_ref.shape[1]
    hd = d // n_heads
    scale = hd ** -0.5 * LOG2E
    x = x_ref[...]
    hc = (x * _rms_scale(x) * g_ref[...]).astype(BF16)
    for c in range(d // nc):
        sl = slice(c * nc, (c + 1) * nc)
        q_ref[:, sl] = jnp.dot(hc, wq_ref[:, sl], preferred_element_type=F32).astype(BF16)
    for h in range(n_heads):
        sl = slice(h * hd, (h + 1) * hd)
        k = kv_ref[:, h * hd:(h + 1) * hd]
        v = kv_ref[:, d + h * hd:d + (h + 1) * hd]
        s = lax.dot_general(q_ref[:, sl], k, (((1,), (1,)), ((), ())), preferred_element_type=F32) * scale
        p = jnp.exp2(s - jnp.max(s, axis=-1, keepdims=True))
        o = jnp.dot(p.astype(BF16), v, preferred_element_type=F32) / jnp.sum(p, axis=-1, keepdims=True)
        a_ref[:, sl] = o.astype(BF16)
    a = a_ref[...]
    for c in range(d // nc):
        sl = slice(c * nc, (c + 1) * nc)
        o_ref[:, sl] = x_ref[:, sl] + jnp.dot(a, wo_ref[:, sl], preferred_element_type=F32)


def cross_block(x, g, kv, w_q, w_o, *, batch, seq, tq, nc, name):
    d = x.shape[1]
    n_mem = kv.shape[0] // batch
    out = pl.pallas_call(
        functools.partial(_cross_kernel, n_heads=N_XHEADS, nc=nc),
        grid=(batch, seq // tq),
        in_specs=[
            pl.BlockSpec((None, tq, d), lambda b, i: (b, i, 0)),
            pl.BlockSpec((1, d), lambda b, i: (0, 0)),
            pl.BlockSpec((None, n_mem, 2 * d), lambda b, i: (b, 0, 0)),
            _resident((d, d)), _resident((d, d)),
        ],
        out_specs=pl.BlockSpec((None, tq, d), lambda b, i: (b, i, 0)),
        out_shape=jax.ShapeDtypeStruct((batch, seq, d), F32),
        scratch_shapes=[pltpu.VMEM((tq, d), BF16), pltpu.VMEM((tq, d), BF16)],
        compiler_params=_params("parallel", "parallel"),
        name=name,
    )(x.reshape(batch, seq, d), g.reshape(1, d), kv.reshape(batch, n_mem, 2 * d), w_q, w_o)
    return out.reshape(batch * seq, d)


def _mlp_kernel(x_ref, g_ref, wu_ref, wd_ref, gf_ref, o_ref, h_ref, *, nc, final_norm):
    f = pl.program_id(1)
    d = o_ref.shape[1]

    @pl.when(f == 0)
    def _():
        x = x_ref[...]
        h_ref[...] = (x * _rms_scale(x) * g_ref[...]).astype(BF16)
        o_ref[...] = x

    up = jnp.dot(h_ref[...], wu_ref[...], preferred_element_type=F32)
    act = jnp.square(jnp.maximum(up, 0.0)).astype(BF16)
    for c in range(d // nc):
        sl = slice(c * nc, (c + 1) * nc)
        o_ref[:, sl] += jnp.dot(act, wd_ref[:, sl], preferred_element_type=F32)

    if final_norm:
        @pl.when(f == pl.num_programs(1) - 1)
        def _():
            y = o_ref[...]
            o_ref[...] = y * _rms_scale(y) * gf_ref[...]


def mlp_block(x, g, w_up, w_down, g_final, *, tm, tf, nc, final_norm, name):
    m, d = x.shape
    d_ff = w_up.shape[1]
    return pl.pallas_call(
        functools.partial(_mlp_kernel, nc=nc, final_norm=final_norm),
        grid=(m // tm, d_ff // tf),
        in_specs=[
            pl.BlockSpec((tm, d), lambda i, f: (i, 0)),
            pl.BlockSpec((1, d), lambda i, f: (0, 0)),
            pl.BlockSpec((d, tf), lambda i, f: (0, f)),
            pl.BlockSpec((tf, d), lambda i, f: (f, 0)),
            pl.BlockSpec((1, d), lambda i, f: (0, 0)),
        ],
        out_specs=pl.BlockSpec((tm, d), lambda i, f: (i, 0)),
        out_shape=jax.ShapeDtypeStruct((m, d), F32),
        scratch_shapes=[pltpu.VMEM((tm, d), BF16)],
        compiler_params=_params("parallel", "arbitrary"),
        name=name,
    )(x, g.reshape(1, d), w_up, w_down, g_final.reshape(1, d))


def kernel(x, mem, mix_norm_g, w_in, q_norm_g, k_norm_g, conv_w, conv_b, lru_w_r, lru_b_r, lru_w_i, lru_b_i, lru_lambda, w_attn_branch, w_rnn_branch, w_mix_out, cross_norm_g, mem_norm_g, w_xq, w_xkv, w_xo, mlp_norm_g, w_up, w_down, final_norm_g):
    batch, seq, d_model = x.shape
    n_mem = mem.shape[1]
    depth = w_in.shape[0]
    attn_w = w_attn_branch.shape[1]
    d_rnn = w_rnn_branch.shape[1]
    kv_w = attn_w // KV_GROUP
    n_blocks, rnn_block = lru_w_r.shape[2], lru_w_r.shape[3]
    assert w_in.shape[2] == attn_w + 2 * kv_w + 2 * d_rnn + 2 * d_model and rnn_block == LANES and depth >= 1

    xf = x.reshape(batch * seq, d_model)
    memf = mem.reshape(batch * n_mem, d_model)
    cos_t, sin_t = rope_tables(seq)

    for l in range(depth):
        tag = f"l{l}_"
        qk, v, u, gates = in_proj(xf, mix_norm_g[l], w_in[l].astype(BF16), cos_t, sin_t, q_norm_g[l], k_norm_g[l],
                                  seq=seq, attn_w=attn_w, kv_w=kv_w, d_rnn=d_rnn, tm=1024, tn=512,
                                  name=tag + "in_proj")
        o_attn = gqa_attention(qk, v, batch=batch, seq=seq, n_q_cols=attn_w, tq=128, kc=512, racc=32,
                               name=tag + "attn")
        w_gates = jnp.concatenate([lru_w_r[l, 0], lru_w_i[l, 0], lru_w_r[l, 1], lru_w_i[l, 1]],
                                  axis=-1).astype(BF16)
        b_gates = jnp.concatenate(
            [b.reshape(n_blocks, 1, rnn_block) for b in (lru_b_r[l, 0], lru_b_i[l, 0], lru_b_r[l, 1], lru_b_i[l, 1])],
            axis=-1)
        o_rnn = rglru_branch(u, gates, conv_w[l], conv_b[l], w_gates, b_gates, lru_lambda[l], batch=batch, seq=seq,
                             name=tag + "rglru")
        xf = merge_mix(o_attn, o_rnn, gates, xf, w_attn_branch[l].astype(BF16), w_rnn_branch[l].astype(BF16),
                       w_mix_out[l].astype(BF16), ga_col=d_rnn, gr_col=d_rnn + d_model, tm=256, nc=512,
                       name=tag + "merge_mix")
        kv = norm_matmul(memf, mem_norm_g[l], w_xkv[l].astype(BF16), out_dtype=BF16, tm=batch * n_mem, tn=1024,
                         name=tag + "xkv")
        xf = cross_block(xf, cross_norm_g[l], kv, w_xq[l].astype(BF16), w_xo[l].astype(BF16), batch=batch, seq=seq,
                         tq=512, nc=512, name=tag + "cross")
        xf = mlp_block(xf, mlp_norm_g[l], w_up[l].astype(BF16), w_down[l].astype(BF16), final_norm_g,
                       tm=1024, tf=512, nc=512, final_norm=(l == depth - 1), name=tag + "mlp")
    return xf.reshape(batch, seq, d_model)
```

```python
import functools
import math

import jax
import jax.numpy as jnp
from jax import lax
from jax.experimental import pallas as pl
from jax.experimental.pallas import tpu as pltpu

F32 = jnp.float32
BF16 = jnp.bfloat16

GRID_W = 64
HEAD_DIM = 128
KV_GROUP = 4
ROPE_THETA = 10000.0
CONV_W = 4
CONV_LEFT = CONV_W // 2
LRU_C = 8.0
N_XHEADS = 4
EPS = 1e-6
LOG2E = math.log2(math.e)

LANES = 128
SUBLANES = 8
VMEM_LIMIT_BYTES = 56 * 1024 * 1024


def _params(*sem):
    return pltpu.CompilerParams(dimension_semantics=sem, vmem_limit_bytes=VMEM_LIMIT_BYTES)


def _resident_layer(shape, layer):
    return pl.BlockSpec((None,) + shape, lambda *_: (layer,) + (0,) * len(shape), pipeline_mode=pl.Buffered(1))


def _rms_scale(x):
    return lax.rsqrt(jnp.mean(x * x, axis=-1, keepdims=True) + EPS)


def _sigmoid(x):
    return 0.5 * jnp.tanh(0.5 * x) + 0.5


def _norm_matmul_kernel(x_ref, g_ref, w_ref, o_ref, h_ref):
    @pl.when(pl.program_id(1) == 0)
    def _():
        x = x_ref[...]
        h_ref[...] = (x * _rms_scale(x) * g_ref[...]).astype(BF16)

    o_ref[...] = jnp.dot(h_ref[...], w_ref[...], preferred_element_type=F32).astype(o_ref.dtype)


def norm_matmul(x, g, w, layer, *, out_dtype, tm, tn, name):
    m, d = x.shape
    n = w.shape[2]
    return pl.pallas_call(
        _norm_matmul_kernel,
        grid=(m // tm, n // tn),
        in_specs=[
            pl.BlockSpec((tm, d), lambda i, j: (i, 0)),
            pl.BlockSpec((1, d), lambda i, j: (0, 0)),
            pl.BlockSpec((None, d, tn), lambda i, j: (layer, 0, j)),
        ],
        out_specs=pl.BlockSpec((tm, tn), lambda i, j: (i, j)),
        out_shape=jax.ShapeDtypeStruct((m, n), out_dtype),
        scratch_shapes=[pltpu.VMEM((tm, d), BF16)],
        compiler_params=_params("parallel", "arbitrary"),
        name=name,
    )(x, g.reshape(1, d), w)


def _qk_proj_kernel(x_ref, g_ref, w_ref, cos_ref, sin_ref, gq_ref, gk_ref, qk_ref, h_ref, *, n_q_heads, q_scale):
    j = pl.program_id(1)

    @pl.when(j == 0)
    def _():
        x = x_ref[...]
        h_ref[...] = (x * _rms_scale(x) * g_ref[...]).astype(BF16)

    y = jnp.dot(h_ref[...], w_ref[...], preferred_element_type=F32)
    heads = y.shape[1] // HEAD_DIM
    gq = gq_ref[...] * q_scale
    gk = gk_ref[...]
    cos = cos_ref[...]
    sin = sin_ref[...]
    lane = lax.broadcasted_iota(jnp.int32, cos.shape, 1)
    first_half = (lane % (HEAD_DIM // 2)) < (HEAD_DIM // 4)
    mean_mat = jnp.full((HEAD_DIM, HEAD_DIM), 1.0 / HEAD_DIM, BF16)
    for h in range(heads):
        sl = slice(h * HEAD_DIM, (h + 1) * HEAD_DIM)
        g = jnp.where(j * heads + h < n_q_heads, gq, gk)
        xh = y[:, sl]
        sq = xh * xh
        sq_hi = sq.astype(BF16)
        sq_lo = (sq - sq_hi.astype(F32)).astype(BF16)
        ms = (jnp.dot(sq_hi, mean_mat, preferred_element_type=F32)
              + jnp.dot(sq_lo, mean_mat, preferred_element_type=F32))
        xn = xh * lax.rsqrt(ms + EPS) * g
        swapped = jnp.where(first_half,
                            pltpu.roll(xn, HEAD_DIM - HEAD_DIM // 4, axis=1),
                            pltpu.roll(xn, HEAD_DIM // 4, axis=1))
        qk_ref[:, sl] = (xn * cos + swapped * sin).astype(qk_ref.dtype)


def qk_proj(x, g, w_qk, layer, cos_t, sin_t, gq, gk, *, seq, attn_w, tm, tn, name):
    m, d = x.shape
    n = w_qk.shape[2]
    assert n % tn == 0 and tn % HEAD_DIM == 0 and seq % tm == 0
    t_blocks = seq // tm
    return pl.pallas_call(
        functools.partial(_qk_proj_kernel, n_q_heads=attn_w // HEAD_DIM, q_scale=HEAD_DIM ** -0.5 * LOG2E),
        grid=(m // tm, n // tn),
        in_specs=[
            pl.BlockSpec((tm, d), lambda i, j: (i, 0)),
            pl.BlockSpec((1, d), lambda i, j: (0, 0)),
            pl.BlockSpec((None, d, tn), lambda i, j: (layer, 0, j)),
            pl.BlockSpec((tm, HEAD_DIM), lambda i, j: (i % t_blocks, 0)),
            pl.BlockSpec((tm, HEAD_DIM), lambda i, j: (i % t_blocks, 0)),
            pl.BlockSpec((1, HEAD_DIM), lambda i, j: (0, 0)),
            pl.BlockSpec((1, HEAD_DIM), lambda i, j: (0, 0)),
        ],
        out_specs=[pl.BlockSpec((tm, tn), lambda i, j: (i, j)),
                   pl.BlockSpec((tm, d), lambda i, j: (i, 0))],
        out_shape=[jax.ShapeDtypeStruct((m, n), BF16), jax.ShapeDtypeStruct((m, d), BF16)],
        compiler_params=_params("parallel", "arbitrary"),
        name=name,
    )(x, g.reshape(1, d), w_qk, cos_t, sin_t, gq.reshape(1, HEAD_DIM), gk.reshape(1, HEAD_DIM))


def _matmul_kernel(a_ref, w_ref, o_ref):
    o_ref[...] = jnp.dot(a_ref[...], w_ref[...], preferred_element_type=F32).astype(o_ref.dtype)


def matmul(a, w, layer, *, out_dtype, tm, tn, name):
    m, k = a.shape
    n = w.shape[2]
    assert m % tm == 0 and n % tn == 0
    return pl.pallas_call(
        _matmul_kernel,
        grid=(m // tm, n // tn),
        in_specs=[pl.BlockSpec((tm, k), lambda i, j: (i, 0)),
                  pl.BlockSpec((None, k, tn), lambda i, j: (layer, 0, j))],
        out_specs=pl.BlockSpec((tm, tn), lambda i, j: (i, j)),
        out_shape=jax.ShapeDtypeStruct((m, n), out_dtype),
        compiler_params=_params("parallel", "arbitrary"),
        name=name,
    )(a, w)


def rope_tables(seq):
    rows_n = seq // GRID_W
    row = jnp.repeat(jnp.arange(rows_n, dtype=F32), GRID_W)
    col = jnp.tile(jnp.arange(GRID_W, dtype=F32), rows_n)
    n_freq = HEAD_DIM // 4
    inv = ROPE_THETA ** (-jnp.arange(n_freq, dtype=F32) / n_freq)
    ang_r = row[:, None] * inv[None, :]
    ang_c = col[:, None] * inv[None, :]
    cos_t = jnp.concatenate([jnp.cos(ang_r), jnp.cos(ang_r), jnp.cos(ang_c), jnp.cos(ang_c)], axis=-1)
    sin_t = jnp.concatenate([-jnp.sin(ang_r), jnp.sin(ang_r), -jnp.sin(ang_c), jnp.sin(ang_c)], axis=-1)
    return cos_t, sin_t


def _qk_phase(q_ref, k_ref, s_ref, m_ref, t, *, tq, kc, racc):
    seq = k_ref.shape[0]
    n = KV_GROUP * tq
    rows_q = pl.ds(pl.multiple_of(t * tq, tq), tq)
    q = jnp.concatenate([q_ref[rows_q, g * HEAD_DIM:(g + 1) * HEAD_DIM] for g in range(KV_GROUP)], axis=0)
    m_acc = jnp.full((racc, n), -jnp.inf, F32)
    for c in range(seq // kc):
        st = lax.dot_general(k_ref[c * kc:(c + 1) * kc, :], q, (((1,), (1,)), ((), ())),
                             preferred_element_type=F32)
        s_ref[c * kc:(c + 1) * kc, :] = st
        for r in range(kc // racc):
            m_acc = jnp.maximum(m_acc, st[r * racc:(r + 1) * racc, :])
    m_ref[...] = jnp.broadcast_to(jnp.max(m_acc, axis=0, keepdims=True), m_ref.shape)


def _pv_phase(s_ref, m_ref, vt_ref, o_ref, t, *, tq, kc, racc):
    seq = s_ref.shape[0]
    n = KV_GROUP * tq
    rows_q = pl.ds(pl.multiple_of(t * tq, tq), tq)
    m = m_ref[0:1, :]
    l_acc = jnp.zeros((racc, n), F32)
    acc = jnp.zeros((HEAD_DIM, n), F32)
    for c in range(seq // kc):
        p = jnp.exp2(s_ref[c * kc:(c + 1) * kc, :] - m)
        for r in range(kc // racc):
            l_acc = l_acc + p[r * racc:(r + 1) * racc, :]
        acc = acc + jnp.dot(vt_ref[:, c * kc:(c + 1) * kc], p.astype(BF16), preferred_element_type=F32)
    out = acc * (1.0 / jnp.sum(l_acc, axis=0, keepdims=True))
    for g in range(KV_GROUP):
        o_ref[rows_q, g * HEAD_DIM:(g + 1) * HEAD_DIM] = out[:, g * tq:(g + 1) * tq].T.astype(o_ref.dtype)


def _attn_kernel(q_ref, k_ref, v_ref, o_ref, s0_ref, s1_ref, m0_ref, m1_ref, vt_ref, *, tq, kc, racc):
    seq = k_ref.shape[0]
    nq = seq // tq
    for c in range(seq // kc):
        vt_ref[:, c * kc:(c + 1) * kc] = v_ref[c * kc:(c + 1) * kc, :].astype(F32).T.astype(BF16)
    qk = functools.partial(_qk_phase, q_ref, k_ref, tq=tq, kc=kc, racc=racc)
    pv = functools.partial(_pv_phase, vt_ref=vt_ref, o_ref=o_ref, tq=tq, kc=kc, racc=racc)
    qk(s0_ref, m0_ref, 0)

    def pair(i, carry):
        t = 2 * i
        qk(s1_ref, m1_ref, t + 1)
        pv(s0_ref, m0_ref, t=t)
        qk(s0_ref, m0_ref, t + 2)
        pv(s1_ref, m1_ref, t=t + 1)
        return carry

    lax.fori_loop(0, nq // 2 - 1, pair, 0)
    qk(s1_ref, m1_ref, nq - 1)
    pv(s0_ref, m0_ref, t=nq - 2)
    pv(s1_ref, m1_ref, t=nq - 1)


def gqa_attention(qk, rest, *, batch, seq, n_q_cols, v_col, tq, kc, racc, name):
    n_kv = n_q_cols // (KV_GROUP * HEAD_DIM)
    group_w = KV_GROUP * HEAD_DIM
    qk3 = qk.reshape(batch, seq, qk.shape[1])
    v3 = rest.reshape(batch, seq, rest.shape[1])
    k_blk0 = n_q_cols // HEAD_DIM
    v_blk0 = v_col // HEAD_DIM
    n = KV_GROUP * tq
    assert (seq // tq) % 2 == 0 and seq % kc == 0 and kc % racc == 0
    out = pl.pallas_call(
        functools.partial(_attn_kernel, tq=tq, kc=kc, racc=racc),
        grid=(batch, n_kv),
        in_specs=[
            pl.BlockSpec((None, seq, group_w), lambda b, j: (b, 0, j)),
            pl.BlockSpec((None, seq, HEAD_DIM), lambda b, j: (b, 0, k_blk0 + j)),
            pl.BlockSpec((None, seq, HEAD_DIM), lambda b, j: (b, 0, v_blk0 + j)),
        ],
        out_specs=pl.BlockSpec((None, seq, group_w), lambda b, j: (b, 0, j)),
        out_shape=jax.ShapeDtypeStruct((batch, seq, n_q_cols), BF16),
        scratch_shapes=[pltpu.VMEM((seq, n), F32), pltpu.VMEM((seq, n), F32),
                        pltpu.VMEM((SUBLANES, n), F32), pltpu.VMEM((SUBLANES, n), F32),
                        pltpu.VMEM((HEAD_DIM, seq), BF16)],
        compiler_params=_params("parallel", "parallel"),
        name=name,
    )(qk3, qk3, v3)
    return out.reshape(batch * seq, n_q_cols)


def _softplus(x):
    e = jnp.exp(-jnp.abs(x))
    u = 1.0 + e
    log1p_e = jnp.where(u == 1.0, e, jnp.log(u) * e / (u - 1.0))
    return jnp.maximum(x, 0.0) + log1p_e


def _scan8(a, b, reverse):
    row = lax.broadcasted_iota(jnp.int32, a.shape, 0)
    for d in (1, 2, 4):
        if reverse:
            valid = row < SUBLANES - d
            shift = SUBLANES - d
        else:
            valid = row >= d
            shift = d
        a_prev = jnp.where(valid, pltpu.roll(a, shift, axis=0), 1.0)
        b_prev = jnp.where(valid, pltpu.roll(b, shift, axis=0), 0.0)
        b = a * b_prev + b
        a = a * a_prev
    return a, b


SQRT_TINY = 1e-30
PITCH_PAD = SUBLANES


def _shift_rows(x, down):
    row = lax.broadcasted_iota(jnp.int32, x.shape, 0)
    if down:
        return jnp.where(row >= 1, pltpu.roll(x, 1, axis=0), 0.0)
    return jnp.where(row < SUBLANES - 1, pltpu.roll(x, SUBLANES - 1, axis=0), 0.0)


def _lru_kernel(u_ref, y_ref, cw_ref, cb_ref, wg_ref, bg_ref, lam_ref, o_ref,
                un_ref, ui_ref, af_ref, bf_ref, ab_ref, bb_ref, pf_ref, hf_ref, pb_ref, hb_ref, hn_ref,
                *, tc, unroll):
    seq, cb = u_ref.shape
    nt = seq // SUBLANES
    pitch = nt + PITCH_PAD
    halo = CONV_LEFT

    for s in range(SUBLANES):
        un_ref[s * pitch:s * pitch + nt, :] = u_ref[s * nt:(s + 1) * nt, :].astype(F32)

    def gather(t, carry):
        ui_ref[pl.ds(pl.multiple_of((t + halo) * SUBLANES, SUBLANES), SUBLANES), :] = \
            un_ref[pl.ds(t, SUBLANES, stride=pitch), :]
        return carry

    lax.fori_loop(0, nt, gather, 0, unroll=unroll)
    for k in range(halo):
        src = (nt + k) * SUBLANES
        ui_ref[k * SUBLANES:(k + 1) * SUBLANES, :] = _shift_rows(ui_ref[src:src + SUBLANES, :], down=True)
    for k in range(CONV_W - 1 - CONV_LEFT):
        src = (halo + k) * SUBLANES
        dst = (halo + nt + k) * SUBLANES
        ui_ref[dst:dst + SUBLANES, :] = _shift_rows(ui_ref[src:src + SUBLANES, :], down=False)

    wg = wg_ref[...] * 0.5
    bg = bg_ref[...] * 0.5
    half_decay = [(-0.5 * LRU_C * LOG2E) * _softplus(-lam_ref[d:d + 1, :]) for d in range(2)]

    for c in range(seq // tc):
        r0 = c * tc
        uf = jnp.broadcast_to(cb_ref[...], (tc, cb))
        for tap in range(CONV_W):
            uf = uf + ui_ref[r0 + tap * SUBLANES:r0 + tap * SUBLANES + tc, :] * cw_ref[tap:tap + 1, :]
        gates = jnp.dot(uf.astype(BF16), wg, preferred_element_type=F32) + bg
        uh = 0.5 * uf
        for d, (a_ref, b_ref) in enumerate(((af_ref, bf_ref), (ab_ref, bb_ref))):
            tr = jnp.tanh(gates[:, (2 * d) * cb:(2 * d + 1) * cb])
            ti = jnp.tanh(gates[:, (2 * d + 1) * cb:(2 * d + 2) * cb])
            a = jnp.exp2(half_decay[d] * tr + half_decay[d])
            iu = uh * ti + uh
            om = 1.0 - a * a
            a_ref[r0:r0 + tc, :] = a
            b_ref[r0:r0 + tc, :] = (om * lax.rsqrt(jnp.maximum(om, SQRT_TINY))) * iu

    def scan_body(t, carry):
        hf, pf, hb, pb = carry
        rf = pl.ds(pl.multiple_of(t * SUBLANES, SUBLANES), SUBLANES)
        rb = pl.ds(pl.multiple_of((nt - 1 - t) * SUBLANES, SUBLANES), SUBLANES)
        a = af_ref[rf, :]
        hf = a * hf + bf_ref[rf, :]
        pf = a * pf
        hf_ref[rf, :] = hf
        pf_ref[rf, :] = pf
        a = ab_ref[rb, :]
        hb = a * hb + bb_ref[rb, :]
        pb = a * pb
        hb_ref[rb, :] = hb
        pb_ref[rb, :] = pb
        return hf, pf, hb, pb

    zero = jnp.zeros((SUBLANES, cb), F32)
    one = jnp.ones((SUBLANES, cb), F32)
    hf, pf, hb, pb = lax.fori_loop(0, nt, scan_body, (zero, one, zero, one), unroll=unroll)
    init_f = _shift_rows(_scan8(pf, hf, reverse=False)[1], down=True)
    init_b = _shift_rows(_scan8(pb, hb, reverse=True)[1], down=False)

    def fix(t, carry):
        r = pl.ds(pl.multiple_of(t * SUBLANES, SUBLANES), SUBLANES)
        h = (hf_ref[r, :] + pf_ref[r, :] * init_f) + (hb_ref[r, :] + pb_ref[r, :] * init_b)
        hn_ref[pl.ds(t, SUBLANES, stride=pitch), :] = h
        return carry

    lax.fori_loop(0, nt, fix, 0, unroll=unroll)

    for s in range(SUBLANES):
        for c in range(nt // tc):
            src = slice(s * pitch + c * tc, s * pitch + (c + 1) * tc)
            dst = slice(s * nt + c * tc, s * nt + (c + 1) * tc)
            o_ref[dst, :] = (hn_ref[src, :] * jax.nn.gelu(y_ref[dst, :].astype(F32))).astype(o_ref.dtype)


def rglru_branch(rest, conv_w, conv_b, w_gates, b_gates, lam, *, batch, seq, d_rnn, u_col, y_col, name):
    cb = LANES
    n_blk = d_rnn // cb
    rest3 = rest.reshape(batch, seq, rest.shape[1])
    u_blk0, y_blk0 = u_col // cb, y_col // cb
    nt = seq // SUBLANES
    padded = SUBLANES * (nt + PITCH_PAD)
    tc = 256
    assert nt % tc == 0
    out = pl.pallas_call(
        functools.partial(_lru_kernel, tc=tc, unroll=8),
        grid=(batch, n_blk),
        in_specs=[
            pl.BlockSpec((None, seq, cb), lambda b, n: (b, 0, u_blk0 + n)),
            pl.BlockSpec((None, seq, cb), lambda b, n: (b, 0, y_blk0 + n)),
            pl.BlockSpec((CONV_W, cb), lambda b, n: (0, n)),
            pl.BlockSpec((1, cb), lambda b, n: (0, n)),
            pl.BlockSpec((None, cb, 4 * cb), lambda b, n: (n, 0, 0)),
            pl.BlockSpec((None, 1, 4 * cb), lambda b, n: (n, 0, 0)),
            pl.BlockSpec((2, cb), lambda b, n: (0, n)),
        ],
        out_specs=pl.BlockSpec((None, seq, cb), lambda b, n: (b, 0, n)),
        out_shape=jax.ShapeDtypeStruct((batch, seq, d_rnn), BF16),
        scratch_shapes=[pltpu.VMEM((padded, cb), F32),
                        pltpu.VMEM((seq + (CONV_W - 1) * SUBLANES, cb), F32)]
                       + [pltpu.VMEM((seq, cb), F32)] * 8 + [pltpu.VMEM((padded, cb), F32)],
        compiler_params=_params("parallel", "parallel"),
        name=name,
    )(rest3, rest3, conv_w, conv_b.reshape(1, d_rnn), w_gates, b_gates, lam)
    return out.reshape(batch * seq, d_rnn)


def _merge_mix_kernel(oa_ref, or_ref, ga_ref, gr_ref, x_ref, wa_ref, wr_ref, wo_ref, o_ref, mg_ref, *, nc):
    d = wo_ref.shape[1]
    oa = oa_ref[...]
    orn = or_ref[...]
    for c in range(d // nc):
        sl = slice(c * nc, (c + 1) * nc)
        ya = jnp.dot(oa, wa_ref[:, sl], preferred_element_type=F32)
        yr = jnp.dot(orn, wr_ref[:, sl], preferred_element_type=F32)
        mg_ref[:, sl] = (_sigmoid(ga_ref[:, sl].astype(F32)) * ya
                         + _sigmoid(gr_ref[:, sl].astype(F32)) * yr).astype(BF16)
    mg = mg_ref[...]
    for c in range(d // nc):
        sl = slice(c * nc, (c + 1) * nc)
        o_ref[:, sl] = x_ref[:, sl] + jnp.dot(mg, wo_ref[:, sl], preferred_element_type=F32)


def merge_mix(o_attn, o_rnn, gates, x, w_a, w_r, w_o, layer, *, ga_col, gr_col, tm, nc, name):
    m, d = x.shape
    assert ga_col % d == 0 and gr_col % d == 0 and w_a.shape[1:] == w_r.shape[1:] == w_o.shape[1:] == (d, d)
    row = lambda i: (i, 0)
    return pl.pallas_call(
        functools.partial(_merge_mix_kernel, nc=nc),
        grid=(m // tm,),
        in_specs=[
            pl.BlockSpec((tm, d), row),
            pl.BlockSpec((tm, d), row),
            pl.BlockSpec((tm, d), lambda i: (i, ga_col // d)),
            pl.BlockSpec((tm, d), lambda i: (i, gr_col // d)),
            pl.BlockSpec((tm, d), row),
            _resident_layer((d, d), layer), _resident_layer((d, d), layer), _resident_layer((d, d), layer),
        ],
        out_specs=pl.BlockSpec((tm, d), row),
        out_shape=jax.ShapeDtypeStruct((m, d), F32),
        scratch_shapes=[pltpu.VMEM((tm, d), BF16)],
        compiler_params=_params("parallel"),
        name=name,
    )(o_attn, o_rnn, gates, gates, x, w_a, w_r, w_o)


def _cross_kernel(x_ref, g_ref, kv_ref, wq_ref, wo_ref, o_ref, q_ref, a_ref, *, n_heads, nc):
    d = x_ref.shape[1]
    hd = d // n_heads
    scale = hd ** -0.5 * LOG2E
    x = x_ref[...]
    hc = (x * _rms_scale(x) * g_ref[...]).astype(BF16)
    for c in range(d // nc):
        sl = slice(c * nc, (c + 1) * nc)
        q_ref[:, sl] = jnp.dot(hc, wq_ref[:, sl], preferred_element_type=F32).astype(BF16)
    for h in range(n_heads):
        sl = slice(h * hd, (h + 1) * hd)
        k = kv_ref[:, h * hd:(h + 1) * hd]
        v = kv_ref[:, d + h * hd:d + (h + 1) * hd]
        s = lax.dot_general(q_ref[:, sl], k, (((1,), (1,)), ((), ())), preferred_element_type=F32) * scale
        p = jnp.exp2(s - jnp.max(s, axis=-1, keepdims=True))
        o = jnp.dot(p.astype(BF16), v, preferred_element_type=F32) / jnp.sum(p, axis=-1, keepdims=True)
        a_ref[:, sl] = o.astype(BF16)
    a = a_ref[...]
    for c in range(d // nc):
        sl = slice(c * nc, (c + 1) * nc)
        o_ref[:, sl] = x_ref[:, sl] + jnp.dot(a, wo_ref[:, sl], preferred_element_type=F32)


def cross_block(x, g, kv, w_q, w_o, layer, *, batch, seq, tq, nc, name):
    d = x.shape[1]
    n_mem = kv.shape[0] // batch
    out = pl.pallas_call(
        functools.partial(_cross_kernel, n_heads=N_XHEADS, nc=nc),
        grid=(batch, seq // tq),
        in_specs=[
            pl.BlockSpec((None, tq, d), lambda b, i: (b, i, 0)),
            pl.BlockSpec((1, d), lambda b, i: (0, 0)),
            pl.BlockSpec((None, n_mem, 2 * d), lambda b, i: (b, 0, 0)),
            _resident_layer((d, d), layer), _resident_layer((d, d), layer),
        ],
        out_specs=pl.BlockSpec((None, tq, d), lambda b, i: (b, i, 0)),
        out_shape=jax.ShapeDtypeStruct((batch, seq, d), F32),
        scratch_shapes=[pltpu.VMEM((tq, d), BF16), pltpu.VMEM((tq, d), BF16)],
        compiler_params=_params("parallel", "parallel"),
        name=name,
    )(x.reshape(batch, seq, d), g.reshape(1, d), kv.reshape(batch, n_mem, 2 * d), w_q, w_o)
    return out.reshape(batch * seq, d)


def _mlp_kernel(x_ref, g_ref, wu_ref, wd_ref, gf_ref, o_ref, h_ref, *, nc, final_norm):
    f = pl.program_id(1)
    d = o_ref.shape[1]

    @pl.when(f == 0)
    def _():
        x = x_ref[...]
        h_ref[...] = (x * _rms_scale(x) * g_ref[...]).astype(BF16)
        o_ref[...] = x

    up = jnp.dot(h_ref[...], wu_ref[...], preferred_element_type=F32)
    act = jnp.square(jnp.maximum(up, 0.0)).astype(BF16)
    for c in range(d // nc):
        sl = slice(c * nc, (c + 1) * nc)
        o_ref[:, sl] += jnp.dot(act, wd_ref[:, sl], preferred_element_type=F32)

    if final_norm:
        @pl.when(f == pl.num_programs(1) - 1)
        def _():
            y = o_ref[...]
            o_ref[...] = y * _rms_scale(y) * gf_ref[...]


def mlp_block(x, g, w_up, w_down, layer, g_final, *, tm, tf, nc, final_norm, name):
    m, d = x.shape
    d_ff = w_up.shape[2]
    return pl.pallas_call(
        functools.partial(_mlp_kernel, nc=nc, final_norm=final_norm),
        grid=(m // tm, d_ff // tf),
        in_specs=[
            pl.BlockSpec((tm, d), lambda i, f: (i, 0)),
            pl.BlockSpec((1, d), lambda i, f: (0, 0)),
            pl.BlockSpec((None, d, tf), lambda i, f: (layer, 0, f)),
            pl.BlockSpec((None, tf, d), lambda i, f: (layer, f, 0)),
            pl.BlockSpec((1, d), lambda i, f: (0, 0)),
        ],
        out_specs=pl.BlockSpec((tm, d), lambda i, f: (i, 0)),
        out_shape=jax.ShapeDtypeStruct((m, d), F32),
        scratch_shapes=[pltpu.VMEM((tm, d), BF16)],
        compiler_params=_params("parallel", "arbitrary"),
        name=name,
    )(x, g.reshape(1, d), w_up, w_down, g_final.reshape(1, d))


def kernel(x, mem, mix_norm_g, w_in, q_norm_g, k_norm_g, conv_w, conv_b, lru_w_r, lru_b_r, lru_w_i, lru_b_i, lru_lambda, w_attn_branch, w_rnn_branch, w_mix_out, cross_norm_g, mem_norm_g, w_xq, w_xkv, w_xo, mlp_norm_g, w_up, w_down, final_norm_g):
    batch, seq, d_model = x.shape
    n_mem = mem.shape[1]
    depth = w_in.shape[0]
    attn_w = w_attn_branch.shape[1]
    d_rnn = w_rnn_branch.shape[1]
    kv_w = attn_w // KV_GROUP
    n_blocks, rnn_block = lru_w_r.shape[2], lru_w_r.shape[3]
    assert w_in.shape[2] == attn_w + 2 * kv_w + 2 * d_rnn + 2 * d_model and rnn_block == LANES and depth >= 1

    xf = x.reshape(batch * seq, d_model)
    memf = mem.reshape(batch * n_mem, d_model)
    cos_t, sin_t = rope_tables(seq)

    qk_w = attn_w + kv_w
    w_qk = w_in[:, :, :qk_w].astype(BF16)
    w_rest = jnp.concatenate([w_in[:, :, qk_w + kv_w:], w_in[:, :, qk_w:qk_w + kv_w]], axis=-1).astype(BF16)
    u_col, y_col, ga_col, gr_col = 0, d_rnn, 2 * d_rnn, 2 * d_rnn + d_model
    v_col = gr_col + d_model
    w_a, w_r, w_o = (w.astype(BF16) for w in (w_attn_branch, w_rnn_branch, w_mix_out))
    w_xq_b, w_xkv_b, w_xo_b, w_up_b, w_down_b = (w.astype(BF16) for w in (w_xq, w_xkv, w_xo, w_up, w_down))

    for l in range(depth):
        tag = f"l{l}_"
        qk, h = qk_proj(xf, mix_norm_g[l], w_qk, l, cos_t, sin_t, q_norm_g[l], k_norm_g[l], seq=seq, attn_w=attn_w,
                        tm=1024, tn=qk_w // 2, name=tag + "qk_proj")
        rest = matmul(h, w_rest, l, out_dtype=BF16, tm=2048, tn=512, name=tag + "rest_proj")
        o_attn = gqa_attention(qk, rest, batch=batch, seq=seq, n_q_cols=attn_w, v_col=v_col, tq=128, kc=512, racc=32,
                               name=tag + "attn")
        w_gates = jnp.concatenate([lru_w_r[l, 0], lru_w_i[l, 0], lru_w_r[l, 1], lru_w_i[l, 1]],
                                  axis=-1).astype(BF16)
        b_gates = jnp.concatenate(
            [b.reshape(n_blocks, 1, rnn_block) for b in (lru_b_r[l, 0], lru_b_i[l, 0], lru_b_r[l, 1], lru_b_i[l, 1])],
            axis=-1)
        o_rnn = rglru_branch(rest, conv_w[l], conv_b[l], w_gates, b_gates, lru_lambda[l], batch=batch, seq=seq,
                             d_rnn=d_rnn, u_col=u_col, y_col=y_col, name=tag + "rglru")
        xf = merge_mix(o_attn, o_rnn, rest, xf, w_a, w_r, w_o, l, ga_col=ga_col, gr_col=gr_col, tm=256, nc=512,
                       name=tag + "merge_mix")
        kv = norm_matmul(memf, mem_norm_g[l], w_xkv_b, l, out_dtype=BF16, tm=batch * n_mem, tn=1024,
                         name=tag + "xkv")
        xf = cross_block(xf, cross_norm_g[l], kv, w_xq_b, w_xo_b, l, batch=batch, seq=seq, tq=512, nc=512,
                         name=tag + "cross")
        xf = mlp_block(xf, mlp_norm_g[l], w_up_b, w_down_b, l, final_norm_g, tm=1024, tf=512, nc=512,
                       final_norm=(l == depth - 1), name=tag + "mlp")
    return xf.reshape(batch, seq, d_model)
```

```python
import functools
import math

import jax
import jax.numpy as jnp
from jax import lax
from jax.experimental import pallas as pl
from jax.experimental.pallas import tpu as pltpu

F32 = jnp.float32
BF16 = jnp.bfloat16

GRID_W = 64
HEAD_DIM = 128
KV_GROUP = 4
ROPE_THETA = 10000.0
CONV_W = 4
CONV_LEFT = CONV_W // 2
LRU_C = 8.0
N_XHEADS = 4
EPS = 1e-6
LOG2E = math.log2(math.e)

LANES = 128
SUBLANES = 8
VMEM_LIMIT_BYTES = 56 * 1024 * 1024


def _params(*sem):
    return pltpu.CompilerParams(dimension_semantics=sem, vmem_limit_bytes=VMEM_LIMIT_BYTES)


def _resident_layer(shape, layer):
    return pl.BlockSpec((None,) + shape, lambda *_: (layer,) + (0,) * len(shape), pipeline_mode=pl.Buffered(1))


def _rms_scale(x):
    return lax.rsqrt(jnp.mean(x * x, axis=-1, keepdims=True) + EPS)


def _sigmoid(x):
    return 0.5 * jnp.tanh(0.5 * x) + 0.5


def _norm_matmul_kernel(x_ref, g_ref, w_ref, o_ref, h_ref):
    @pl.when(pl.program_id(1) == 0)
    def _():
        x = x_ref[...]
        h_ref[...] = (x * _rms_scale(x) * g_ref[...]).astype(BF16)

    o_ref[...] = jnp.dot(h_ref[...], w_ref[...], preferred_element_type=F32).astype(o_ref.dtype)


def norm_matmul(x, g, w, layer, *, out_dtype, tm, tn, name):
    m, d = x.shape
    n = w.shape[2]
    return pl.pallas_call(
        _norm_matmul_kernel,
        grid=(m // tm, n // tn),
        in_specs=[
            pl.BlockSpec((tm, d), lambda i, j: (i, 0)),
            pl.BlockSpec((1, d), lambda i, j: (0, 0)),
            pl.BlockSpec((None, d, tn), lambda i, j: (layer, 0, j)),
        ],
        out_specs=pl.BlockSpec((tm, tn), lambda i, j: (i, j)),
        out_shape=jax.ShapeDtypeStruct((m, n), out_dtype),
        scratch_shapes=[pltpu.VMEM((tm, d), BF16)],
        compiler_params=_params("parallel", "arbitrary"),
        name=name,
    )(x, g.reshape(1, d), w)


def _qk_proj_kernel(x_ref, g_ref, w_ref, cos_ref, sin_ref, gq_ref, gk_ref, qk_ref, h_ref, *, n_q_heads, q_scale):
    j = pl.program_id(1)

    @pl.when(j == 0)
    def _():
        x = x_ref[...]
        h_ref[...] = (x * _rms_scale(x) * g_ref[...]).astype(BF16)

    y = jnp.dot(h_ref[...], w_ref[...], preferred_element_type=F32)
    heads = y.shape[1] // HEAD_DIM
    gq = gq_ref[...] * q_scale
    gk = gk_ref[...]
    cos = cos_ref[...]
    sin = sin_ref[...]
    lane = lax.broadcasted_iota(jnp.int32, cos.shape, 1)
    first_half = (lane % (HEAD_DIM // 2)) < (HEAD_DIM // 4)
    mean_mat = jnp.full((HEAD_DIM, HEAD_DIM), 1.0 / HEAD_DIM, BF16)
    for h in range(heads):
        sl = slice(h * HEAD_DIM, (h + 1) * HEAD_DIM)
        g = jnp.where(j * heads + h < n_q_heads, gq, gk)
        xh = y[:, sl]
        sq = xh * xh
        sq_hi = sq.astype(BF16)
        sq_lo = (sq - sq_hi.astype(F32)).astype(BF16)
        ms = (jnp.dot(sq_hi, mean_mat, preferred_element_type=F32)
              + jnp.dot(sq_lo, mean_mat, preferred_element_type=F32))
        xn = xh * lax.rsqrt(ms + EPS) * g
        swapped = jnp.where(first_half,
                            pltpu.roll(xn, HEAD_DIM - HEAD_DIM // 4, axis=1),
                            pltpu.roll(xn, HEAD_DIM // 4, axis=1))
        qk_ref[:, sl] = (xn * cos + swapped * sin).astype(qk_ref.dtype)


def qk_proj(x, g, w_qk, layer, cos_t, sin_t, gq, gk, *, seq, attn_w, tm, tn, name):
    m, d = x.shape
    n = w_qk.shape[2]
    assert n % tn == 0 and tn % HEAD_DIM == 0 and seq % tm == 0
    t_blocks = seq // tm
    return pl.pallas_call(
        functools.partial(_qk_proj_kernel, n_q_heads=attn_w // HEAD_DIM, q_scale=HEAD_DIM ** -0.5 * LOG2E),
        grid=(m // tm, n // tn),
        in_specs=[
            pl.BlockSpec((tm, d), lambda i, j: (i, 0)),
            pl.BlockSpec((1, d), lambda i, j: (0, 0)),
            pl.BlockSpec((None, d, tn), lambda i, j: (layer, 0, j)),
            pl.BlockSpec((tm, HEAD_DIM), lambda i, j: (i % t_blocks, 0)),
            pl.BlockSpec((tm, HEAD_DIM), lambda i, j: (i % t_blocks, 0)),
            pl.BlockSpec((1, HEAD_DIM), lambda i, j: (0, 0)),
            pl.BlockSpec((1, HEAD_DIM), lambda i, j: (0, 0)),
        ],
        out_specs=[pl.BlockSpec((tm, tn), lambda i, j: (i, j)),
                   pl.BlockSpec((tm, d), lambda i, j: (i, 0))],
        out_shape=[jax.ShapeDtypeStruct((m, n), BF16), jax.ShapeDtypeStruct((m, d), BF16)],
        compiler_params=_params("parallel", "arbitrary"),
        name=name,
    )(x, g.reshape(1, d), w_qk, cos_t, sin_t, gq.reshape(1, HEAD_DIM), gk.reshape(1, HEAD_DIM))


def _matmul_kernel(a_ref, w_ref, o_ref):
    o_ref[...] = jnp.dot(a_ref[...], w_ref[...], preferred_element_type=F32).astype(o_ref.dtype)


def matmul(a, w, layer, *, out_dtype, tm, tn, name):
    m, k = a.shape
    n = w.shape[2]
    assert m % tm == 0 and n % tn == 0
    return pl.pallas_call(
        _matmul_kernel,
        grid=(m // tm, n // tn),
        in_specs=[pl.BlockSpec((tm, k), lambda i, j: (i, 0)),
                  pl.BlockSpec((None, k, tn), lambda i, j: (layer, 0, j))],
        out_specs=pl.BlockSpec((tm, tn), lambda i, j: (i, j)),
        out_shape=jax.ShapeDtypeStruct((m, n), out_dtype),
        compiler_params=_params("parallel", "arbitrary"),
        name=name,
    )(a, w)


def rope_tables(seq):
    rows_n = seq // GRID_W
    row = jnp.repeat(jnp.arange(rows_n, dtype=F32), GRID_W)
    col = jnp.tile(jnp.arange(GRID_W, dtype=F32), rows_n)
    n_freq = HEAD_DIM // 4
    inv = ROPE_THETA ** (-jnp.arange(n_freq, dtype=F32) / n_freq)
    ang_r = row[:, None] * inv[None, :]
    ang_c = col[:, None] * inv[None, :]
    cos_t = jnp.concatenate([jnp.cos(ang_r), jnp.cos(ang_r), jnp.cos(ang_c), jnp.cos(ang_c)], axis=-1)
    sin_t = jnp.concatenate([-jnp.sin(ang_r), jnp.sin(ang_r), -jnp.sin(ang_c), jnp.sin(ang_c)], axis=-1)
    return cos_t, sin_t


def _qk_phase(q_ref, k_ref, s_ref, m_ref, t, *, tq, kc, racc):
    seq = k_ref.shape[0]
    n = KV_GROUP * tq
    rows_q = pl.ds(pl.multiple_of(t * tq, tq), tq)
    q = jnp.concatenate([q_ref[rows_q, g * HEAD_DIM:(g + 1) * HEAD_DIM] for g in range(KV_GROUP)], axis=0)
    m_acc = jnp.full((racc, n), -jnp.inf, F32)
    for c in range(seq // kc):
        st = lax.dot_general(k_ref[c * kc:(c + 1) * kc, :], q, (((1,), (1,)), ((), ())),
                             preferred_element_type=F32)
        s_ref[c * kc:(c + 1) * kc, :] = st
        for r in range(kc // racc):
            m_acc = jnp.maximum(m_acc, st[r * racc:(r + 1) * racc, :])
    m_ref[...] = jnp.broadcast_to(jnp.max(m_acc, axis=0, keepdims=True), m_ref.shape)


def _pv_phase(s_ref, m_ref, vt_ref, o_ref, t, *, tq, kc, racc):
    seq = s_ref.shape[0]
    n = KV_GROUP * tq
    rows_q = pl.ds(pl.multiple_of(t * tq, tq), tq)
    m = m_ref[0:1, :]
    l_acc = jnp.zeros((racc, n), F32)
    acc = jnp.zeros((HEAD_DIM, n), F32)
    for c in range(seq // kc):
        p = jnp.exp2(s_ref[c * kc:(c + 1) * kc, :] - m)
        for r in range(kc // racc):
            l_acc = l_acc + p[r * racc:(r + 1) * racc, :]
        acc = acc + jnp.dot(vt_ref[:, c * kc:(c + 1) * kc], p.astype(BF16), preferred_element_type=F32)
    out = acc * (1.0 / jnp.sum(l_acc, axis=0, keepdims=True))
    for g in range(KV_GROUP):
        o_ref[rows_q, g * HEAD_DIM:(g + 1) * HEAD_DIM] = out[:, g * tq:(g + 1) * tq].T.astype(o_ref.dtype)


def _attn_kernel(q_ref, k_ref, v_ref, o_ref, s0_ref, s1_ref, m0_ref, m1_ref, vt_ref, *, tq, kc, racc):
    seq = k_ref.shape[0]
    nq = seq // tq
    for c in range(seq // kc):
        vt_ref[:, c * kc:(c + 1) * kc] = v_ref[c * kc:(c + 1) * kc, :].astype(F32).T.astype(BF16)
    qk = functools.partial(_qk_phase, q_ref, k_ref, tq=tq, kc=kc, racc=racc)
    pv = functools.partial(_pv_phase, vt_ref=vt_ref, o_ref=o_ref, tq=tq, kc=kc, racc=racc)
    qk(s0_ref, m0_ref, 0)

    def pair(i, carry):
        t = 2 * i
        qk(s1_ref, m1_ref, t + 1)
        pv(s0_ref, m0_ref, t=t)
        qk(s0_ref, m0_ref, t + 2)
        pv(s1_ref, m1_ref, t=t + 1)
        return carry

    lax.fori_loop(0, nq // 2 - 1, pair, 0)
    qk(s1_ref, m1_ref, nq - 1)
    pv(s0_ref, m0_ref, t=nq - 2)
    pv(s1_ref, m1_ref, t=nq - 1)


def gqa_attention(qk, rest, *, batch, seq, n_q_cols, v_col, tq, kc, racc, name):
    n_kv = n_q_cols // (KV_GROUP * HEAD_DIM)
    group_w = KV_GROUP * HEAD_DIM
    qk3 = qk.reshape(batch, seq, qk.shape[1])
    v3 = rest.reshape(batch, seq, rest.shape[1])
    k_blk0 = n_q_cols // HEAD_DIM
    v_blk0 = v_col // HEAD_DIM
    n = KV_GROUP * tq
    assert (seq // tq) % 2 == 0 and seq % kc == 0 and kc % racc == 0
    out = pl.pallas_call(
        functools.partial(_attn_kernel, tq=tq, kc=kc, racc=racc),
        grid=(batch, n_kv),
        in_specs=[
            pl.BlockSpec((None, seq, group_w), lambda b, j: (b, 0, j)),
            pl.BlockSpec((None, seq, HEAD_DIM), lambda b, j: (b, 0, k_blk0 + j)),
            pl.BlockSpec((None, seq, HEAD_DIM), lambda b, j: (b, 0, v_blk0 + j)),
        ],
        out_specs=pl.BlockSpec((None, seq, group_w), lambda b, j: (b, 0, j)),
        out_shape=jax.ShapeDtypeStruct((batch, seq, n_q_cols), BF16),
        scratch_shapes=[pltpu.VMEM((seq, n), F32), pltpu.VMEM((seq, n), F32),
                        pltpu.VMEM((SUBLANES, n), F32), pltpu.VMEM((SUBLANES, n), F32),
                        pltpu.VMEM((HEAD_DIM, seq), BF16)],
        compiler_params=_params("parallel", "parallel"),
        name=name,
    )(qk3, qk3, v3)
    return out.reshape(batch * seq, n_q_cols)


def _softplus(x):
    e = jnp.exp(-jnp.abs(x))
    u = 1.0 + e
    log1p_e = jnp.where(u == 1.0, e, jnp.log(u) * e / (u - 1.0))
    return jnp.maximum(x, 0.0) + log1p_e


def _scan8(a, b, reverse):
    row = lax.broadcasted_iota(jnp.int32, a.shape, 0)
    for d in (1, 2, 4):
        if reverse:
            valid = row < SUBLANES - d
            shift = SUBLANES - d
        else:
            valid = row >= d
            shift = d
        a_prev = jnp.where(valid, pltpu.roll(a, shift, axis=0), 1.0)
        b_prev = jnp.where(valid, pltpu.roll(b, shift, axis=0), 0.0)
        b = a * b_prev + b
        a = a * a_prev
    return a, b


SQRT_TINY = 1e-30
PITCH_PAD = SUBLANES


def _shift_rows(x, down):
    row = lax.broadcasted_iota(jnp.int32, x.shape, 0)
    if down:
        return jnp.where(row >= 1, pltpu.roll(x, 1, axis=0), 0.0)
    return jnp.where(row < SUBLANES - 1, pltpu.roll(x, SUBLANES - 1, axis=0), 0.0)


def _lru_kernel(u_ref, y_ref, cw_ref, cb_ref, wg_ref, bg_ref, lam_ref, o_ref,
                un_ref, ui_ref, af_ref, bf_ref, ab_ref, bb_ref, pf_ref, hf_ref, pb_ref, hb_ref, hn_ref,
                *, tc, unroll):
    seq, cb = u_ref.shape
    nt = seq // SUBLANES
    pitch = nt + PITCH_PAD
    halo = CONV_LEFT

    for s in range(SUBLANES):
        un_ref[s * pitch:s * pitch + nt, :] = u_ref[s * nt:(s + 1) * nt, :].astype(F32)

    def gather(t, carry):
        ui_ref[pl.ds(pl.multiple_of((t + halo) * SUBLANES, SUBLANES), SUBLANES), :] = \
            un_ref[pl.ds(t, SUBLANES, stride=pitch), :]
        return carry

    lax.fori_loop(0, nt, gather, 0, unroll=unroll)
    for k in range(halo):
        src = (nt + k) * SUBLANES
        ui_ref[k * SUBLANES:(k + 1) * SUBLANES, :] = _shift_rows(ui_ref[src:src + SUBLANES, :], down=True)
    for k in range(CONV_W - 1 - CONV_LEFT):
        src = (halo + k) * SUBLANES
        dst = (halo + nt + k) * SUBLANES
        ui_ref[dst:dst + SUBLANES, :] = _shift_rows(ui_ref[src:src + SUBLANES, :], down=False)

    wg = wg_ref[...] * 0.5
    bg = bg_ref[...] * 0.5
    half_decay = [(-0.5 * LRU_C * LOG2E) * _softplus(-lam_ref[d:d + 1, :]) for d in range(2)]

    for c in range(seq // tc):
        r0 = c * tc
        uf = jnp.broadcast_to(cb_ref[...], (tc, cb))
        for tap in range(CONV_W):
            uf = uf + ui_ref[r0 + tap * SUBLANES:r0 + tap * SUBLANES + tc, :] * cw_ref[tap:tap + 1, :]
        gates = jnp.dot(uf.astype(BF16), wg, preferred_element_type=F32) + bg
        uh = 0.5 * uf
        for d, (a_ref, b_ref) in enumerate(((af_ref, bf_ref), (ab_ref, bb_ref))):
            tr = jnp.tanh(gates[:, (2 * d) * cb:(2 * d + 1) * cb])
            ti = jnp.tanh(gates[:, (2 * d + 1) * cb:(2 * d + 2) * cb])
            a = jnp.exp2(half_decay[d] * tr + half_decay[d])
            iu = uh * ti + uh
            om = 1.0 - a * a
            a_ref[r0:r0 + tc, :] = a
            b_ref[r0:r0 + tc, :] = (om * lax.rsqrt(jnp.maximum(om, SQRT_TINY))) * iu

    nh = nt // 2

    def tile(t):
        return pl.ds(pl.multiple_of(t * SUBLANES, SUBLANES), SUBLANES)

    def step(a_ref, b_ref, p_ref, h_ref, r, h, p):
        a = a_ref[r, :]
        h = a * h + b_ref[r, :]
        p = a * p
        h_ref[r, :] = h
        p_ref[r, :] = p
        return h, p

    def scan_body(t, carry):
        hf0, pf0, hf1, pf1, hb1, pb1, hb0, pb0 = carry
        hf0, pf0 = step(af_ref, bf_ref, pf_ref, hf_ref, tile(t), hf0, pf0)
        hf1, pf1 = step(af_ref, bf_ref, pf_ref, hf_ref, tile(nh + t), hf1, pf1)
        hb1, pb1 = step(ab_ref, bb_ref, pb_ref, hb_ref, tile(nt - 1 - t), hb1, pb1)
        hb0, pb0 = step(ab_ref, bb_ref, pb_ref, hb_ref, tile(nh - 1 - t), hb0, pb0)
        return hf0, pf0, hf1, pf1, hb1, pb1, hb0, pb0

    zero = jnp.zeros((SUBLANES, cb), F32)
    one = jnp.ones((SUBLANES, cb), F32)
    hf0, pf0, hf1, pf1, hb1, pb1, hb0, pb0 = lax.fori_loop(0, nh, scan_body, (zero, one) * 4, unroll=unroll)
    init_f0 = _shift_rows(_scan8(pf0 * pf1, pf1 * hf0 + hf1, reverse=False)[1], down=True)
    init_f1 = pf0 * init_f0 + hf0
    init_b1 = _shift_rows(_scan8(pb1 * pb0, pb0 * hb1 + hb0, reverse=True)[1], down=False)
    init_b0 = pb1 * init_b1 + hb1

    def fix(t, carry):
        for t_half, init_f, init_b in ((t, init_f0, init_b0), (nh + t, init_f1, init_b1)):
            r = tile(t_half)
            h = (hf_ref[r, :] + pf_ref[r, :] * init_f) + (hb_ref[r, :] + pb_ref[r, :] * init_b)
            hn_ref[pl.ds(t_half, SUBLANES, stride=pitch), :] = h
        return carry

    lax.fori_loop(0, nh, fix, 0, unroll=unroll)

    for s in range(SUBLANES):
        for c in range(nt // tc):
            src = slice(s * pitch + c * tc, s * pitch + (c + 1) * tc)
            dst = slice(s * nt + c * tc, s * nt + (c + 1) * tc)
            o_ref[dst, :] = (hn_ref[src, :] * jax.nn.gelu(y_ref[dst, :].astype(F32))).astype(o_ref.dtype)


def rglru_branch(rest, conv_w, conv_b, w_gates, b_gates, lam, *, batch, seq, d_rnn, u_col, y_col, name):
    cb = LANES
    n_blk = d_rnn // cb
    rest3 = rest.reshape(batch, seq, rest.shape[1])
    u_blk0, y_blk0 = u_col // cb, y_col // cb
    nt = seq // SUBLANES
    padded = SUBLANES * (nt + PITCH_PAD)
    tc = 256
    assert nt % tc == 0
    out = pl.pallas_call(
        functools.partial(_lru_kernel, tc=tc, unroll=8),
        grid=(batch, n_blk),
        in_specs=[
            pl.BlockSpec((None, seq, cb), lambda b, n: (b, 0, u_blk0 + n)),
            pl.BlockSpec((None, seq, cb), lambda b, n: (b, 0, y_blk0 + n)),
            pl.BlockSpec((CONV_W, cb), lambda b, n: (0, n)),
            pl.BlockSpec((1, cb), lambda b, n: (0, n)),
            pl.BlockSpec((None, cb, 4 * cb), lambda b, n: (n, 0, 0)),
            pl.BlockSpec((None, 1, 4 * cb), lambda b, n: (n, 0, 0)),
            pl.BlockSpec((2, cb), lambda b, n: (0, n)),
        ],
        out_specs=pl.BlockSpec((None, seq, cb), lambda b, n: (b, 0, n)),
        out_shape=jax.ShapeDtypeStruct((batch, seq, d_rnn), BF16),
        scratch_shapes=[pltpu.VMEM((padded, cb), F32),
                        pltpu.VMEM((seq + (CONV_W - 1) * SUBLANES, cb), F32)]
                       + [pltpu.VMEM((seq, cb), F32)] * 8 + [pltpu.VMEM((padded, cb), F32)],
        compiler_params=_params("parallel", "parallel"),
        name=name,
    )(rest3, rest3, conv_w, conv_b.reshape(1, d_rnn), w_gates, b_gates, lam)
    return out.reshape(batch * seq, d_rnn)


def _merge_mix_kernel(oa_ref, or_ref, x_ref, wa_ref, wr_ref, wo_ref, *rest, nc):
    d = wo_ref.shape[1]
    n_chunks = d // nc
    ga_refs, gr_refs = rest[:n_chunks], rest[n_chunks:2 * n_chunks]
    o_ref, mg_ref = rest[2 * n_chunks:]
    oa = oa_ref[...]
    orn = or_ref[...]
    for c in range(n_chunks):
        sl = slice(c * nc, (c + 1) * nc)
        ya = jnp.dot(oa, wa_ref[:, sl], preferred_element_type=F32)
        yr = jnp.dot(orn, wr_ref[:, sl], preferred_element_type=F32)
        mg_ref[:, sl] = (_sigmoid(ga_refs[c][...].astype(F32)) * ya
                         + _sigmoid(gr_refs[c][...].astype(F32)) * yr).astype(BF16)
    mg = mg_ref[...]
    for c in range(n_chunks):
        sl = slice(c * nc, (c + 1) * nc)
        o_ref[:, sl] = x_ref[:, sl] + jnp.dot(mg, wo_ref[:, sl], preferred_element_type=F32)


def merge_mix(o_attn, o_rnn, gates, x, w_a, w_r, w_o, layer, *, ga_col, gr_col, tm, nc, name):
    m, d = x.shape
    assert ga_col % nc == 0 and gr_col % nc == 0 and w_a.shape[1:] == w_r.shape[1:] == w_o.shape[1:] == (d, d)
    row = lambda i: (i, 0)
    gate_specs = [pl.BlockSpec((tm, nc), functools.partial(lambda i, blk: (i, blk), blk=col // nc + c))
                  for col in (ga_col, gr_col) for c in range(d // nc)]
    return pl.pallas_call(
        functools.partial(_merge_mix_kernel, nc=nc),
        grid=(m // tm,),
        in_specs=[
            pl.BlockSpec((tm, d), row),
            pl.BlockSpec((tm, d), row),
            pl.BlockSpec((tm, d), row),
            _resident_layer((d, d), layer), _resident_layer((d, d), layer), _resident_layer((d, d), layer),
        ] + gate_specs,
        out_specs=pl.BlockSpec((tm, d), row),
        out_shape=jax.ShapeDtypeStruct((m, d), F32),
        scratch_shapes=[pltpu.VMEM((tm, d), BF16)],
        compiler_params=_params("parallel"),
        name=name,
    )(o_attn, o_rnn, x, w_a, w_r, w_o, *([gates] * (2 * (d // nc))))


def _cross_kernel(x_ref, g_ref, kv_ref, wq_ref, wo_ref, o_ref, q_ref, a_ref, *, n_heads, nc):
    d = x_ref.shape[1]
    hd = d // n_heads
    scale = hd ** -0.5 * LOG2E
    x = x_ref[...]
    hc = (x * _rms_scale(x) * g_ref[...]).astype(BF16)
    for c in range(d // nc):
        sl = slice(c * nc, (c + 1) * nc)
        q_ref[:, sl] = jnp.dot(hc, wq_ref[:, sl], preferred_element_type=F32).astype(BF16)
    for h in range(n_heads):
        sl = slice(h * hd, (h + 1) * hd)
        k = kv_ref[:, h * hd:(h + 1) * hd]
        v = kv_ref[:, d + h * hd:d + (h + 1) * hd]
        s = lax.dot_general(q_ref[:, sl], k, (((1,), (1,)), ((), ())), preferred_element_type=F32) * scale
        p = jnp.exp2(s - jnp.max(s, axis=-1, keepdims=True))
        o = jnp.dot(p.astype(BF16), v, preferred_element_type=F32) / jnp.sum(p, axis=-1, keepdims=True)
        a_ref[:, sl] = o.astype(BF16)
    a = a_ref[...]
    for c in range(d // nc):
        sl = slice(c * nc, (c + 1) * nc)
        o_ref[:, sl] = x_ref[:, sl] + jnp.dot(a, wo_ref[:, sl], preferred_element_type=F32)


def cross_block(x, g, kv, w_q, w_o, layer, *, batch, seq, tq, nc, name):
    d = x.shape[1]
    n_mem = kv.shape[0] // batch
    out = pl.pallas_call(
        functools.partial(_cross_kernel, n_heads=N_XHEADS, nc=nc),
        grid=(batch, seq // tq),
        in_specs=[
            pl.BlockSpec((None, tq, d), lambda b, i: (b, i, 0)),
            pl.BlockSpec((1, d), lambda b, i: (0, 0)),
            pl.BlockSpec((None, n_mem, 2 * d), lambda b, i: (b, 0, 0)),
            _resident_layer((d, d), layer), _resident_layer((d, d), layer),
        ],
        out_specs=pl.BlockSpec((None, tq, d), lambda b, i: (b, i, 0)),
        out_shape=jax.ShapeDtypeStruct((batch, seq, d), F32),
        scratch_shapes=[pltpu.VMEM((tq, d), BF16), pltpu.VMEM((tq, d), BF16)],
        compiler_params=_params("parallel", "parallel"),
        name=name,
    )(x.reshape(batch, seq, d), g.reshape(1, d), kv.reshape(batch, n_mem, 2 * d), w_q, w_o)
    return out.reshape(batch * seq, d)


def _mlp_kernel(x_ref, g_ref, wu_ref, wd_ref, gf_ref, o_ref, h_ref, *, nc, final_norm):
    f = pl.program_id(1)
    d = o_ref.shape[1]

    @pl.when(f == 0)
    def _():
        x = x_ref[...]
        h_ref[...] = (x * _rms_scale(x) * g_ref[...]).astype(BF16)
        o_ref[...] = x

    up = jnp.dot(h_ref[...], wu_ref[...], preferred_element_type=F32)
    act = jnp.square(jnp.maximum(up, 0.0)).astype(BF16)
    for c in range(d // nc):
        sl = slice(c * nc, (c + 1) * nc)
        o_ref[:, sl] += jnp.dot(act, wd_ref[:, sl], preferred_element_type=F32)

    if final_norm:
        @pl.when(f == pl.num_programs(1) - 1)
        def _():
            y = o_ref[...]
            o_ref[...] = y * _rms_scale(y) * gf_ref[...]


def mlp_block(x, g, w_up, w_down, layer, g_final, *, tm, tf, nc, final_norm, name):
    m, d = x.shape
    d_ff = w_up.shape[2]
    return pl.pallas_call(
        functools.partial(_mlp_kernel, nc=nc, final_norm=final_norm),
        grid=(m // tm, d_ff // tf),
        in_specs=[
            pl.BlockSpec((tm, d), lambda i, f: (i, 0)),
            pl.BlockSpec((1, d), lambda i, f: (0, 0)),
            pl.BlockSpec((None, d, tf), lambda i, f: (layer, 0, f)),
            pl.BlockSpec((None, tf, d), lambda i, f: (layer, f, 0)),
            pl.BlockSpec((1, d), lambda i, f: (0, 0)),
        ],
        out_specs=pl.BlockSpec((tm, d), lambda i, f: (i, 0)),
        out_shape=jax.ShapeDtypeStruct((m, d), F32),
        scratch_shapes=[pltpu.VMEM((tm, d), BF16)],
        compiler_params=_params("parallel", "arbitrary"),
        name=name,
    )(x, g.reshape(1, d), w_up, w_down, g_final.reshape(1, d))


def kernel(x, mem, mix_norm_g, w_in, q_norm_g, k_norm_g, conv_w, conv_b, lru_w_r, lru_b_r, lru_w_i, lru_b_i, lru_lambda, w_attn_branch, w_rnn_branch, w_mix_out, cross_norm_g, mem_norm_g, w_xq, w_xkv, w_xo, mlp_norm_g, w_up, w_down, final_norm_g):
    batch, seq, d_model = x.shape
    n_mem = mem.shape[1]
    depth = w_in.shape[0]
    attn_w = w_attn_branch.shape[1]
    d_rnn = w_rnn_branch.shape[1]
    kv_w = attn_w // KV_GROUP
    n_blocks, rnn_block = lru_w_r.shape[2], lru_w_r.shape[3]
    assert w_in.shape[2] == attn_w + 2 * kv_w + 2 * d_rnn + 2 * d_model and rnn_block == LANES and depth >= 1

    xf = x.reshape(batch * seq, d_model)
    memf = mem.reshape(batch * n_mem, d_model)
    cos_t, sin_t = rope_tables(seq)

    qk_w = attn_w + kv_w
    w_qk = w_in[:, :, :qk_w].astype(BF16)
    w_rest = w_in[:, :, qk_w:].astype(BF16)
    v_col, u_col = 0, kv_w
    y_col = u_col + d_rnn
    ga_col = y_col + d_rnn
    gr_col = ga_col + d_model
    w_a, w_r, w_o = (w.astype(BF16) for w in (w_attn_branch, w_rnn_branch, w_mix_out))
    w_xq_b, w_xkv_b, w_xo_b, w_up_b, w_down_b = (w.astype(BF16) for w in (w_xq, w_xkv, w_xo, w_up, w_down))

    for l in range(depth):
        tag = f"l{l}_"
        qk, h = qk_proj(xf, mix_norm_g[l], w_qk, l, cos_t, sin_t, q_norm_g[l], k_norm_g[l], seq=seq, attn_w=attn_w,
                        tm=1024, tn=qk_w // 2, name=tag + "qk_proj")
        rest = matmul(h, w_rest, l, out_dtype=BF16, tm=2048, tn=512, name=tag + "rest_proj")
        o_attn = gqa_attention(qk, rest, batch=batch, seq=seq, n_q_cols=attn_w, v_col=v_col, tq=64, kc=512, racc=32,
                               name=tag + "attn")
        w_gates = jnp.concatenate([lru_w_r[l, 0], lru_w_i[l, 0], lru_w_r[l, 1], lru_w_i[l, 1]],
                                  axis=-1).astype(BF16)
        b_gates = jnp.concatenate(
            [b.reshape(n_blocks, 1, rnn_block) for b in (lru_b_r[l, 0], lru_b_i[l, 0], lru_b_r[l, 1], lru_b_i[l, 1])],
            axis=-1)
        o_rnn = rglru_branch(rest, conv_w[l], conv_b[l], w_gates, b_gates, lru_lambda[l], batch=batch, seq=seq,
                             d_rnn=d_rnn, u_col=u_col, y_col=y_col, name=tag + "rglru")
        xf = merge_mix(o_attn, o_rnn, rest, xf, w_a, w_r, w_o, l, ga_col=ga_col, gr_col=gr_col, tm=256, nc=512,
                       name=tag + "merge_mix")
        kv = norm_matmul(memf, mem_norm_g[l], w_xkv_b, l, out_dtype=BF16, tm=batch * n_mem, tn=1024,
                         name=tag + "xkv")
        xf = cross_block(xf, cross_norm_g[l], kv, w_xq_b, w_xo_b, l, batch=batch, seq=seq, tq=512, nc=512,
                         name=tag + "cross")
        xf = mlp_block(xf, mlp_norm_g[l], w_up_b, w_down_b, l, final_norm_g, tm=1024, tf=512, nc=512,
                       final_norm=(l == depth - 1), name=tag + "mlp")
    return xf.reshape(batch, seq, d_model)
```

```python
import functools
import math

import jax
import jax.numpy as jnp
from jax import lax
from jax.experimental import pallas as pl
from jax.experimental.pallas import tpu as pltpu

F32 = jnp.float32
BF16 = jnp.bfloat16

GRID_W = 64
HEAD_DIM = 128
KV_GROUP = 4
ROPE_THETA = 10000.0
CONV_W = 4
CONV_LEFT = CONV_W // 2
LRU_C = 8.0
N_XHEADS = 4
EPS = 1e-6
LOG2E = math.log2(math.e)
GELU_C0 = math.sqrt(2.0 / math.pi)
GELU_C1 = 0.044715 * GELU_C0

LANES = 128
SUBLANES = 8
VMEM_LIMIT_BYTES = 56 * 1024 * 1024


def _params(*sem):
    return pltpu.CompilerParams(dimension_semantics=sem, vmem_limit_bytes=VMEM_LIMIT_BYTES)


def _resident_layer(shape, layer):
    return pl.BlockSpec((None,) + shape, lambda *_: (layer,) + (0,) * len(shape), pipeline_mode=pl.Buffered(1))


def _rms_scale(x):
    return lax.rsqrt(jnp.mean(x * x, axis=-1, keepdims=True) + EPS)


def _sigmoid(x):
    return 0.5 * jnp.tanh(0.5 * x) + 0.5


def _norm_matmul_kernel(x_ref, g_ref, w_ref, o_ref, h_ref):
    @pl.when(pl.program_id(1) == 0)
    def _():
        x = x_ref[...]
        h_ref[...] = (x * _rms_scale(x) * g_ref[...]).astype(BF16)

    o_ref[...] = jnp.dot(h_ref[...], w_ref[...], preferred_element_type=F32).astype(o_ref.dtype)


def norm_matmul(x, g, w, layer, *, out_dtype, tm, tn, name):
    m, d = x.shape
    n = w.shape[2]
    return pl.pallas_call(
        _norm_matmul_kernel,
        grid=(m // tm, n // tn),
        in_specs=[
            pl.BlockSpec((tm, d), lambda i, j: (i, 0)),
            pl.BlockSpec((1, d), lambda i, j: (0, 0)),
            pl.BlockSpec((None, d, tn), lambda i, j: (layer, 0, j)),
        ],
        out_specs=pl.BlockSpec((tm, tn), lambda i, j: (i, j)),
        out_shape=jax.ShapeDtypeStruct((m, n), out_dtype),
        scratch_shapes=[pltpu.VMEM((tm, d), BF16)],
        compiler_params=_params("parallel", "arbitrary"),
        name=name,
    )(x, g.reshape(1, d), w)


def _qk_proj_kernel(x_ref, g_ref, w_ref, cos_ref, sin_ref, gq_ref, gk_ref, qk_ref, h_ref, *, n_q_heads, q_scale):
    j = pl.program_id(1)

    @pl.when(j == 0)
    def _():
        x = x_ref[...]
        h_ref[...] = (x * _rms_scale(x) * g_ref[...]).astype(BF16)

    y = jnp.dot(h_ref[...], w_ref[...], preferred_element_type=F32)
    heads = y.shape[1] // HEAD_DIM
    gq = gq_ref[...] * q_scale
    gk = gk_ref[...]
    cos = cos_ref[...]
    sin = sin_ref[...]
    lane = lax.broadcasted_iota(jnp.int32, cos.shape, 1)
    first_half = (lane % (HEAD_DIM // 2)) < (HEAD_DIM // 4)
    mean_mat = jnp.full((HEAD_DIM, HEAD_DIM), 1.0 / HEAD_DIM, BF16)
    for h in range(heads):
        sl = slice(h * HEAD_DIM, (h + 1) * HEAD_DIM)
        g = jnp.where(j * heads + h < n_q_heads, gq, gk)
        xh = y[:, sl]
        sq = xh * xh
        sq_hi = sq.astype(BF16)
        sq_lo = (sq - sq_hi.astype(F32)).astype(BF16)
        ms = (jnp.dot(sq_hi, mean_mat, preferred_element_type=F32)
              + jnp.dot(sq_lo, mean_mat, preferred_element_type=F32))
        xn = xh * lax.rsqrt(ms + EPS) * g
        swapped = jnp.where(first_half,
                            pltpu.roll(xn, HEAD_DIM - HEAD_DIM // 4, axis=1),
                            pltpu.roll(xn, HEAD_DIM // 4, axis=1))
        qk_ref[:, sl] = (xn * cos + swapped * sin).astype(qk_ref.dtype)


def qk_proj(x, g, w_qk, layer, cos_t, sin_t, gq, gk, *, seq, attn_w, n, tm, tn, name):
    m, d = x.shape
    assert n % tn == 0 and tn % HEAD_DIM == 0 and seq % tm == 0
    t_blocks = seq // tm
    return pl.pallas_call(
        functools.partial(_qk_proj_kernel, n_q_heads=attn_w // HEAD_DIM, q_scale=HEAD_DIM ** -0.5 * LOG2E),
        grid=(m // tm, n // tn),
        in_specs=[
            pl.BlockSpec((tm, d), lambda i, j: (i, 0)),
            pl.BlockSpec((1, d), lambda i, j: (0, 0)),
            pl.BlockSpec((None, d, tn), lambda i, j: (layer, 0, j)),
            pl.BlockSpec((tm, HEAD_DIM), lambda i, j: (i % t_blocks, 0)),
            pl.BlockSpec((tm, HEAD_DIM), lambda i, j: (i % t_blocks, 0)),
            pl.BlockSpec((1, HEAD_DIM), lambda i, j: (0, 0)),
            pl.BlockSpec((1, HEAD_DIM), lambda i, j: (0, 0)),
        ],
        out_specs=[pl.BlockSpec((tm, tn), lambda i, j: (i, j)),
                   pl.BlockSpec((tm, d), lambda i, j: (i, 0))],
        out_shape=[jax.ShapeDtypeStruct((m, n), BF16), jax.ShapeDtypeStruct((m, d), BF16)],
        compiler_params=_params("parallel", "arbitrary"),
        name=name,
    )(x, g.reshape(1, d), w_qk, cos_t, sin_t, gq.reshape(1, HEAD_DIM), gk.reshape(1, HEAD_DIM))


def _matmul_kernel(a_ref, w_ref, o_ref):
    o_ref[...] = jnp.dot(a_ref[...], w_ref[...], preferred_element_type=F32).astype(o_ref.dtype)


def matmul(a, w, layer, *, col0, out_dtype, tm, tn, name):
    m, k = a.shape
    n = w.shape[2] - col0
    assert m % tm == 0 and n % tn == 0 and col0 % tn == 0
    blk0 = col0 // tn
    return pl.pallas_call(
        _matmul_kernel,
        grid=(m // tm, n // tn),
        in_specs=[pl.BlockSpec((tm, k), lambda i, j: (i, 0)),
                  pl.BlockSpec((None, k, tn), lambda i, j: (layer, 0, blk0 + j))],
        out_specs=pl.BlockSpec((tm, tn), lambda i, j: (i, j)),
        out_shape=jax.ShapeDtypeStruct((m, n), out_dtype),
        compiler_params=_params("parallel", "arbitrary"),
        name=name,
    )(a, w)


def rope_tables(seq):
    rows_n = seq // GRID_W
    row = jnp.repeat(jnp.arange(rows_n, dtype=F32), GRID_W)
    col = jnp.tile(jnp.arange(GRID_W, dtype=F32), rows_n)
    n_freq = HEAD_DIM // 4
    inv = ROPE_THETA ** (-jnp.arange(n_freq, dtype=F32) / n_freq)
    ang_r = row[:, None] * inv[None, :]
    ang_c = col[:, None] * inv[None, :]
    cos_t = jnp.concatenate([jnp.cos(ang_r), jnp.cos(ang_r), jnp.cos(ang_c), jnp.cos(ang_c)], axis=-1)
    sin_t = jnp.concatenate([-jnp.sin(ang_r), jnp.sin(ang_r), -jnp.sin(ang_c), jnp.sin(ang_c)], axis=-1)
    return cos_t, sin_t


def _qk_phase(q_ref, k_ref, s_ref, m_ref, t, *, tq, kc, racc):
    seq = k_ref.shape[0]
    n = KV_GROUP * tq
    rows_q = pl.ds(pl.multiple_of(t * tq, tq), tq)
    q = jnp.concatenate([q_ref[rows_q, g * HEAD_DIM:(g + 1) * HEAD_DIM] for g in range(KV_GROUP)], axis=0)
    m_acc = jnp.full((racc, n), -jnp.inf, F32)
    for c in range(seq // kc):
        st = lax.dot_general(k_ref[c * kc:(c + 1) * kc, :], q, (((1,), (1,)), ((), ())),
                             preferred_element_type=F32)
        s_ref[c * kc:(c + 1) * kc, :] = st
        for r in range(kc // racc):
            m_acc = jnp.maximum(m_acc, st[r * racc:(r + 1) * racc, :])
    m_ref[...] = jnp.broadcast_to(jnp.max(m_acc, axis=0, keepdims=True), m_ref.shape)


def _pv_phase(s_ref, m_ref, vt_ref, o_ref, t, *, tq, kc, racc):
    seq = s_ref.shape[0]
    n = KV_GROUP * tq
    rows_q = pl.ds(pl.multiple_of(t * tq, tq), tq)
    m = m_ref[0:1, :]
    l_acc = jnp.zeros((racc, n), F32)
    acc = jnp.zeros((HEAD_DIM, n), F32)
    for c in range(seq // kc):
        p = jnp.exp2(s_ref[c * kc:(c + 1) * kc, :] - m)
        for r in range(kc // racc):
            l_acc = l_acc + p[r * racc:(r + 1) * racc, :]
        acc = acc + jnp.dot(vt_ref[:, c * kc:(c + 1) * kc], p.astype(BF16), preferred_element_type=F32)
    out = acc * (1.0 / jnp.sum(l_acc, axis=0, keepdims=True))
    for g in range(KV_GROUP):
        o_ref[rows_q, g * HEAD_DIM:(g + 1) * HEAD_DIM] = out[:, g * tq:(g + 1) * tq].T.astype(o_ref.dtype)


def _attn_kernel(q_ref, k_ref, v_ref, o_ref, s0_ref, s1_ref, m0_ref, m1_ref, vt_ref, *, tq, kc, racc):
    seq = k_ref.shape[0]
    nq = seq // tq
    for c in range(seq // kc):
        vt_ref[:, c * kc:(c + 1) * kc] = v_ref[c * kc:(c + 1) * kc, :].astype(F32).T.astype(BF16)
    qk = functools.partial(_qk_phase, q_ref, k_ref, tq=tq, kc=kc, racc=racc)
    pv = functools.partial(_pv_phase, vt_ref=vt_ref, o_ref=o_ref, tq=tq, kc=kc, racc=racc)
    qk(s0_ref, m0_ref, 0)

    def pair(i, carry):
        t = 2 * i
        qk(s1_ref, m1_ref, t + 1)
        pv(s0_ref, m0_ref, t=t)
        qk(s0_ref, m0_ref, t + 2)
        pv(s1_ref, m1_ref, t=t + 1)
        return carry

    lax.fori_loop(0, nq // 2 - 1, pair, 0)
    qk(s1_ref, m1_ref, nq - 1)
    pv(s0_ref, m0_ref, t=nq - 2)
    pv(s1_ref, m1_ref, t=nq - 1)


def gqa_attention(qk, rest, *, batch, seq, n_q_cols, v_col, tq, kc, racc, name):
    n_kv = n_q_cols // (KV_GROUP * HEAD_DIM)
    group_w = KV_GROUP * HEAD_DIM
    qk3 = qk.reshape(batch, seq, qk.shape[1])
    v3 = rest.reshape(batch, seq, rest.shape[1])
    k_blk0 = n_q_cols // HEAD_DIM
    v_blk0 = v_col // HEAD_DIM
    n = KV_GROUP * tq
    assert (seq // tq) % 2 == 0 and seq % kc == 0 and kc % racc == 0
    out = pl.pallas_call(
        functools.partial(_attn_kernel, tq=tq, kc=kc, racc=racc),
        grid=(batch, n_kv),
        in_specs=[
            pl.BlockSpec((None, seq, group_w), lambda b, j: (b, 0, j)),
            pl.BlockSpec((None, seq, HEAD_DIM), lambda b, j: (b, 0, k_blk0 + j)),
            pl.BlockSpec((None, seq, HEAD_DIM), lambda b, j: (b, 0, v_blk0 + j)),
        ],
        out_specs=pl.BlockSpec((None, seq, group_w), lambda b, j: (b, 0, j)),
        out_shape=jax.ShapeDtypeStruct((batch, seq, n_q_cols), BF16),
        scratch_shapes=[pltpu.VMEM((seq, n), F32), pltpu.VMEM((seq, n), F32),
                        pltpu.VMEM((SUBLANES, n), F32), pltpu.VMEM((SUBLANES, n), F32),
                        pltpu.VMEM((HEAD_DIM, seq), BF16)],
        compiler_params=_params("parallel", "parallel"),
        name=name,
    )(qk3, qk3, v3)
    return out.reshape(batch * seq, n_q_cols)


def _softplus(x):
    e = jnp.exp(-jnp.abs(x))
    u = 1.0 + e
    log1p_e = jnp.where(u == 1.0, e, jnp.log(u) * e / (u - 1.0))
    return jnp.maximum(x, 0.0) + log1p_e


def _scan8(a, b, reverse):
    row = lax.broadcasted_iota(jnp.int32, a.shape, 0)
    for d in (1, 2, 4):
        if reverse:
            valid = row < SUBLANES - d
            shift = SUBLANES - d
        else:
            valid = row >= d
            shift = d
        a_prev = jnp.where(valid, pltpu.roll(a, shift, axis=0), 1.0)
        b_prev = jnp.where(valid, pltpu.roll(b, shift, axis=0), 0.0)
        b = a * b_prev + b
        a = a * a_prev
    return a, b


SQRT_TINY = 1e-30
PITCH_PAD = SUBLANES


def _shift_rows(x, down):
    row = lax.broadcasted_iota(jnp.int32, x.shape, 0)
    if down:
        return jnp.where(row >= 1, pltpu.roll(x, 1, axis=0), 0.0)
    return jnp.where(row < SUBLANES - 1, pltpu.roll(x, SUBLANES - 1, axis=0), 0.0)


def _lru_kernel(u_ref, y_ref, cw_ref, cb_ref, wg_ref, bg_ref, lam_ref, o_ref,
                un_ref, ui_ref, af_ref, bf_ref, ab_ref, bb_ref, pf_ref, hf_ref, pb_ref, hb_ref, hn_ref,
                *, tc, unroll):
    seq, cb = u_ref.shape
    nt = seq // SUBLANES
    pitch = nt + PITCH_PAD
    halo = CONV_LEFT

    for s in range(SUBLANES):
        un_ref[s * pitch:s * pitch + nt, :] = u_ref[s * nt:(s + 1) * nt, :].astype(F32)

    def gather(t, carry):
        ui_ref[pl.ds(pl.multiple_of((t + halo) * SUBLANES, SUBLANES), SUBLANES), :] = \
            un_ref[pl.ds(t, SUBLANES, stride=pitch), :]
        return carry

    lax.fori_loop(0, nt, gather, 0, unroll=unroll)
    for k in range(halo):
        src = (nt + k) * SUBLANES
        ui_ref[k * SUBLANES:(k + 1) * SUBLANES, :] = _shift_rows(ui_ref[src:src + SUBLANES, :], down=True)
    for k in range(CONV_W - 1 - CONV_LEFT):
        src = (halo + k) * SUBLANES
        dst = (halo + nt + k) * SUBLANES
        ui_ref[dst:dst + SUBLANES, :] = _shift_rows(ui_ref[src:src + SUBLANES, :], down=False)

    wg = wg_ref[...] * 0.5
    bg = bg_ref[...] * 0.5
    bg_hi = bg.astype(BF16).astype(F32)
    bg_lo = (bg - bg_hi).astype(BF16).astype(F32)
    w_row = lax.broadcasted_iota(jnp.int32, wg.shape, 0)
    w_bias = jnp.where(w_row == 0, bg_hi, jnp.where(w_row == 1, bg_lo, 0.0)).astype(BF16)
    w_aug = jnp.concatenate([wg, w_bias], axis=0)
    ones_cols = jnp.where(lax.broadcasted_iota(jnp.int32, (tc, cb), 1) < 2, 1.0, 0.0).astype(BF16)
    half_decay = [(-0.5 * LRU_C * LOG2E) * _softplus(-lam_ref[d:d + 1, :]) for d in range(2)]

    for c in range(seq // tc):
        r0 = c * tc
        uf = jnp.broadcast_to(cb_ref[...], (tc, cb))
        for tap in range(CONV_W):
            uf = uf + ui_ref[r0 + tap * SUBLANES:r0 + tap * SUBLANES + tc, :] * cw_ref[tap:tap + 1, :]
        gates = jnp.dot(jnp.concatenate([uf.astype(BF16), ones_cols], axis=1), w_aug,
                        preferred_element_type=F32)
        uh = 0.5 * uf
        for d, (a_ref, b_ref) in enumerate(((af_ref, bf_ref), (ab_ref, bb_ref))):
            tr = jnp.tanh(gates[:, (2 * d) * cb:(2 * d + 1) * cb])
            ti = jnp.tanh(gates[:, (2 * d + 1) * cb:(2 * d + 2) * cb])
            a = jnp.exp2(half_decay[d] * tr + half_decay[d])
            iu = uh * ti + uh
            om = 1.0 - a * a
            a_ref[r0:r0 + tc, :] = a
            b_ref[r0:r0 + tc, :] = (om * lax.rsqrt(jnp.maximum(om, SQRT_TINY))) * iu

    nh = nt // 2

    def tile(t):
        return pl.ds(pl.multiple_of(t * SUBLANES, SUBLANES), SUBLANES)

    def step(a_ref, b_ref, p_ref, h_ref, r, h, p):
        a = a_ref[r, :]
        h = a * h + b_ref[r, :]
        p = a * p
        h_ref[r, :] = h
        p_ref[r, :] = p
        return h, p

    def scan_body(t, carry):
        hf0, pf0, hf1, pf1, hb1, pb1, hb0, pb0 = carry
        hf0, pf0 = step(af_ref, bf_ref, pf_ref, hf_ref, tile(t), hf0, pf0)
        hf1, pf1 = step(af_ref, bf_ref, pf_ref, hf_ref, tile(nh + t), hf1, pf1)
        hb1, pb1 = step(ab_ref, bb_ref, pb_ref, hb_ref, tile(nt - 1 - t), hb1, pb1)
        hb0, pb0 = step(ab_ref, bb_ref, pb_ref, hb_ref, tile(nh - 1 - t), hb0, pb0)
        return hf0, pf0, hf1, pf1, hb1, pb1, hb0, pb0

    zero = jnp.zeros((SUBLANES, cb), F32)
    one = jnp.ones((SUBLANES, cb), F32)
    hf0, pf0, hf1, pf1, hb1, pb1, hb0, pb0 = lax.fori_loop(0, nh, scan_body, (zero, one) * 4, unroll=unroll)
    init_f0 = _shift_rows(_scan8(pf0 * pf1, pf1 * hf0 + hf1, reverse=False)[1], down=True)
    init_f1 = pf0 * init_f0 + hf0
    init_b1 = _shift_rows(_scan8(pb1 * pb0, pb0 * hb1 + hb0, reverse=True)[1], down=False)
    init_b0 = pb1 * init_b1 + hb1

    def fix(t, carry):
        for t_half, init_f, init_b in ((t, init_f0, init_b0), (nh + t, init_f1, init_b1)):
            r = tile(t_half)
            h = (hf_ref[r, :] + pf_ref[r, :] * init_f) + (hb_ref[r, :] + pb_ref[r, :] * init_b)
            hn_ref[pl.ds(t_half, SUBLANES, stride=pitch), :] = h
        return carry

    lax.fori_loop(0, nh, fix, 0, unroll=unroll)

    for s in range(SUBLANES):
        for c in range(nt // tc):
            src = slice(s * pitch + c * tc, s * pitch + (c + 1) * tc)
            dst = slice(s * nt + c * tc, s * nt + (c + 1) * tc)
            yv = y_ref[dst, :].astype(F32)
            th = jnp.tanh(yv * (GELU_C0 + GELU_C1 * (yv * yv)))
            half_y = 0.5 * yv
            o_ref[dst, :] = (hn_ref[src, :] * (half_y * th + half_y)).astype(o_ref.dtype)


def rglru_branch(rest, conv_w, conv_b, w_gates, b_gates, lam, *, batch, seq, d_rnn, u_col, y_col, name):
    cb = LANES
    n_blk = d_rnn // cb
    rest3 = rest.reshape(batch, seq, rest.shape[1])
    u_blk0, y_blk0 = u_col // cb, y_col // cb
    nt = seq // SUBLANES
    padded = SUBLANES * (nt + PITCH_PAD)
    tc = 256
    assert nt % tc == 0
    out = pl.pallas_call(
        functools.partial(_lru_kernel, tc=tc, unroll=8),
        grid=(batch, n_blk),
        in_specs=[
            pl.BlockSpec((None, seq, cb), lambda b, n: (b, 0, u_blk0 + n)),
            pl.BlockSpec((None, seq, cb), lambda b, n: (b, 0, y_blk0 + n)),
            pl.BlockSpec((CONV_W, cb), lambda b, n: (0, n)),
            pl.BlockSpec((1, cb), lambda b, n: (0, n)),
            pl.BlockSpec((None, cb, 4 * cb), lambda b, n: (n, 0, 0)),
            pl.BlockSpec((None, 1, 4 * cb), lambda b, n: (n, 0, 0)),
            pl.BlockSpec((2, cb), lambda b, n: (0, n)),
        ],
        out_specs=pl.BlockSpec((None, seq, cb), lambda b, n: (b, 0, n)),
        out_shape=jax.ShapeDtypeStruct((batch, seq, d_rnn), BF16),
        scratch_shapes=[pltpu.VMEM((padded, cb), F32),
                        pltpu.VMEM((seq + (CONV_W - 1) * SUBLANES, cb), F32)]
                       + [pltpu.VMEM((seq, cb), F32)] * 8 + [pltpu.VMEM((padded, cb), F32)],
        compiler_params=_params("parallel", "parallel"),
        name=name,
    )(rest3, rest3, conv_w, conv_b.reshape(1, d_rnn), w_gates, b_gates, lam)
    return out.reshape(batch * seq, d_rnn)


def _merge_mix_kernel(oa_ref, or_ref, x_ref, wa_ref, wr_ref, wo_ref, *rest, nc):
    d = wo_ref.shape[1]
    n_chunks = d // nc
    ga_refs, gr_refs = rest[:n_chunks], rest[n_chunks:2 * n_chunks]
    o_ref, mg_ref = rest[2 * n_chunks:]
    oa = oa_ref[...]
    orn = or_ref[...]
    for c in range(n_chunks):
        sl = slice(c * nc, (c + 1) * nc)
        ya = jnp.dot(oa, wa_ref[:, sl], preferred_element_type=F32)
        yr = jnp.dot(orn, wr_ref[:, sl], preferred_element_type=F32)
        mg_ref[:, sl] = (_sigmoid(ga_refs[c][...].astype(F32)) * ya
                         + _sigmoid(gr_refs[c][...].astype(F32)) * yr).astype(BF16)
    mg = mg_ref[...]
    for c in range(n_chunks):
        sl = slice(c * nc, (c + 1) * nc)
        o_ref[:, sl] = x_ref[:, sl] + jnp.dot(mg, wo_ref[:, sl], preferred_element_type=F32)


def merge_mix(o_attn, o_rnn, gates, x, w_a, w_r, w_o, layer, *, ga_col, gr_col, tm, nc, name):
    m, d = x.shape
    assert ga_col % nc == 0 and gr_col % nc == 0 and w_a.shape[1:] == w_r.shape[1:] == w_o.shape[1:] == (d, d)
    row = lambda i: (i, 0)
    gate_specs = [pl.BlockSpec((tm, nc), functools.partial(lambda i, blk: (i, blk), blk=col // nc + c))
                  for col in (ga_col, gr_col) for c in range(d // nc)]
    return pl.pallas_call(
        functools.partial(_merge_mix_kernel, nc=nc),
        grid=(m // tm,),
        in_specs=[
            pl.BlockSpec((tm, d), row),
            pl.BlockSpec((tm, d), row),
            pl.BlockSpec((tm, d), row),
            _resident_layer((d, d), layer), _resident_layer((d, d), layer), _resident_layer((d, d), layer),
        ] + gate_specs,
        out_specs=pl.BlockSpec((tm, d), row),
        out_shape=jax.ShapeDtypeStruct((m, d), F32),
        scratch_shapes=[pltpu.VMEM((tm, d), BF16)],
        compiler_params=_params("parallel"),
        name=name,
    )(o_attn, o_rnn, x, w_a, w_r, w_o, *([gates] * (2 * (d // nc))))


def _cross_kernel(x_ref, g_ref, kv_ref, wq_ref, wo_ref, o_ref, q_ref, a_ref, *, n_heads, nc):
    d = x_ref.shape[1]
    hd = d // n_heads
    scale = hd ** -0.5 * LOG2E
    x = x_ref[...]
    hc = (x * _rms_scale(x) * g_ref[...]).astype(BF16)
    for c in range(d // nc):
        sl = slice(c * nc, (c + 1) * nc)
        q_ref[:, sl] = jnp.dot(hc, wq_ref[:, sl], preferred_element_type=F32).astype(BF16)
    for h in range(n_heads):
        sl = slice(h * hd, (h + 1) * hd)
        k = kv_ref[:, h * hd:(h + 1) * hd]
        v = kv_ref[:, d + h * hd:d + (h + 1) * hd]
        s = lax.dot_general(q_ref[:, sl], k, (((1,), (1,)), ((), ())), preferred_element_type=F32) * scale
        p = jnp.exp2(s - jnp.max(s, axis=-1, keepdims=True))
        o = jnp.dot(p.astype(BF16), v, preferred_element_type=F32) / jnp.sum(p, axis=-1, keepdims=True)
        a_ref[:, sl] = o.astype(BF16)
    a = a_ref[...]
    for c in range(d // nc):
        sl = slice(c * nc, (c + 1) * nc)
        o_ref[:, sl] = x_ref[:, sl] + jnp.dot(a, wo_ref[:, sl], preferred_element_type=F32)


def cross_block(x, g, kv, w_q, w_o, layer, *, batch, seq, tq, nc, name):
    d = x.shape[1]
    n_mem = kv.shape[0] // batch
    out = pl.pallas_call(
        functools.partial(_cross_kernel, n_heads=N_XHEADS, nc=nc),
        grid=(batch, seq // tq),
        in_specs=[
            pl.BlockSpec((None, tq, d), lambda b, i: (b, i, 0)),
            pl.BlockSpec((1, d), lambda b, i: (0, 0)),
            pl.BlockSpec((None, n_mem, 2 * d), lambda b, i: (b, 0, 0)),
            _resident_layer((d, d), layer), _resident_layer((d, d), layer),
        ],
        out_specs=pl.BlockSpec((None, tq, d), lambda b, i: (b, i, 0)),
        out_shape=jax.ShapeDtypeStruct((batch, seq, d), F32),
        scratch_shapes=[pltpu.VMEM((tq, d), BF16), pltpu.VMEM((tq, d), BF16)],
        compiler_params=_params("parallel", "parallel"),
        name=name,
    )(x.reshape(batch, seq, d), g.reshape(1, d), kv.reshape(batch, n_mem, 2 * d), w_q, w_o)
    return out.reshape(batch * seq, d)


def _mlp_kernel(x_ref, g_ref, wu_ref, wd_ref, gf_ref, o_ref, h_ref, *, nc, final_norm):
    f = pl.program_id(1)
    d = o_ref.shape[1]

    @pl.when(f == 0)
    def _():
        x = x_ref[...]
        h_ref[...] = (x * _rms_scale(x) * g_ref[...]).astype(BF16)
        o_ref[...] = x

    up = jnp.dot(h_ref[...], wu_ref[...], preferred_element_type=F32)
    act = jnp.square(jnp.maximum(up, 0.0)).astype(BF16)
    for c in range(d // nc):
        sl = slice(c * nc, (c + 1) * nc)
        o_ref[:, sl] += jnp.dot(act, wd_ref[:, sl], preferred_element_type=F32)

    if final_norm:
        @pl.when(f == pl.num_programs(1) - 1)
        def _():
            y = o_ref[...]
            o_ref[...] = y * _rms_scale(y) * gf_ref[...]


def mlp_block(x, g, w_up, w_down, layer, g_final, *, tm, tf, nc, final_norm, name):
    m, d = x.shape
    d_ff = w_up.shape[2]
    return pl.pallas_call(
        functools.partial(_mlp_kernel, nc=nc, final_norm=final_norm),
        grid=(m // tm, d_ff // tf),
        in_specs=[
            pl.BlockSpec((tm, d), lambda i, f: (i, 0)),
            pl.BlockSpec((1, d), lambda i, f: (0, 0)),
            pl.BlockSpec((None, d, tf), lambda i, f: (layer, 0, f)),
            pl.BlockSpec((None, tf, d), lambda i, f: (layer, f, 0)),
            pl.BlockSpec((1, d), lambda i, f: (0, 0)),
        ],
        out_specs=pl.BlockSpec((tm, d), lambda i, f: (i, 0)),
        out_shape=jax.ShapeDtypeStruct((m, d), F32),
        scratch_shapes=[pltpu.VMEM((tm, d), BF16)],
        compiler_params=_params("parallel", "arbitrary"),
        name=name,
    )(x, g.reshape(1, d), w_up, w_down, g_final.reshape(1, d))


def kernel(x, mem, mix_norm_g, w_in, q_norm_g, k_norm_g, conv_w, conv_b, lru_w_r, lru_b_r, lru_w_i, lru_b_i, lru_lambda, w_attn_branch, w_rnn_branch, w_mix_out, cross_norm_g, mem_norm_g, w_xq, w_xkv, w_xo, mlp_norm_g, w_up, w_down, final_norm_g):
    batch, seq, d_model = x.shape
    n_mem = mem.shape[1]
    depth = w_in.shape[0]
    attn_w = w_attn_branch.shape[1]
    d_rnn = w_rnn_branch.shape[1]
    kv_w = attn_w // KV_GROUP
    n_blocks, rnn_block = lru_w_r.shape[2], lru_w_r.shape[3]
    assert w_in.shape[2] == attn_w + 2 * kv_w + 2 * d_rnn + 2 * d_model and rnn_block == LANES and depth >= 1

    xf = x.reshape(batch * seq, d_model)
    memf = mem.reshape(batch * n_mem, d_model)
    cos_t, sin_t = rope_tables(seq)

    qk_w = attn_w + kv_w
    w_in_b = w_in.astype(BF16)
    v_col, u_col = 0, kv_w
    y_col = u_col + d_rnn
    ga_col = y_col + d_rnn
    gr_col = ga_col + d_model
    w_a, w_r, w_o = (w.astype(BF16) for w in (w_attn_branch, w_rnn_branch, w_mix_out))
    w_xq_b, w_xkv_b, w_xo_b, w_up_b, w_down_b = (w.astype(BF16) for w in (w_xq, w_xkv, w_xo, w_up, w_down))

    for l in range(depth):
        tag = f"l{l}_"
        qk, h = qk_proj(xf, mix_norm_g[l], w_in_b, l, cos_t, sin_t, q_norm_g[l], k_norm_g[l], seq=seq, attn_w=attn_w,
                        n=qk_w, tm=1024, tn=qk_w // 2, name=tag + "qk_proj")
        rest = matmul(h, w_in_b, l, col0=qk_w, out_dtype=BF16, tm=2048, tn=512, name=tag + "rest_proj")
        o_attn = gqa_attention(qk, rest, batch=batch, seq=seq, n_q_cols=attn_w, v_col=v_col, tq=64, kc=512, racc=32,
                               name=tag + "attn")
        w_gates = jnp.concatenate([lru_w_r[l, 0], lru_w_i[l, 0], lru_w_r[l, 1], lru_w_i[l, 1]],
                                  axis=-1).astype(BF16)
        b_gates = jnp.concatenate(
            [b.reshape(n_blocks, 1, rnn_block) for b in (lru_b_r[l, 0], lru_b_i[l, 0], lru_b_r[l, 1], lru_b_i[l, 1])],
            axis=-1)
        o_rnn = rglru_branch(rest, conv_w[l], conv_b[l], w_gates, b_gates, lru_lambda[l], batch=batch, seq=seq,
                             d_rnn=d_rnn, u_col=u_col, y_col=y_col, name=tag + "rglru")
        xf = merge_mix(o_attn, o_rnn, rest, xf, w_a, w_r, w_o, l, ga_col=ga_col, gr_col=gr_col, tm=256, nc=512,
                       name=tag + "merge_mix")
        kv = norm_matmul(memf, mem_norm_g[l], w_xkv_b, l, out_dtype=BF16, tm=batch * n_mem, tn=1024,
                         name=tag + "xkv")
        xf = cross_block(xf, cross_norm_g[l], kv, w_xq_b, w_xo_b, l, batch=batch, seq=seq, tq=512, nc=512,
                         name=tag + "cross")
        xf = mlp_block(xf, mlp_norm_g[l], w_up_b, w_down_b, l, final_norm_g, tm=1024, tf=512, nc=512,
                       final_norm=(l == depth - 1), name=tag + "mlp")
    return xf.reshape(batch, seq, d_model)
```

```python
import functools
import math

import jax
import jax.numpy as jnp
from jax import lax
from jax.experimental import pallas as pl
from jax.experimental.pallas import tpu as pltpu

F32 = jnp.float32
BF16 = jnp.bfloat16

GRID_W = 64
HEAD_DIM = 128
KV_GROUP = 4
ROPE_THETA = 10000.0
CONV_W = 4
CONV_LEFT = CONV_W // 2
LRU_C = 8.0
N_XHEADS = 4
EPS = 1e-6
LOG2E = math.log2(math.e)
GELU_C0 = math.sqrt(2.0 / math.pi)
GELU_C1 = 0.044715 * GELU_C0

LANES = 128
SUBLANES = 8
VMEM_LIMIT_BYTES = 56 * 1024 * 1024


def _params(*sem):
    return pltpu.CompilerParams(dimension_semantics=sem, vmem_limit_bytes=VMEM_LIMIT_BYTES)


def _resident_layer(shape, layer):
    return pl.BlockSpec((None,) + shape, lambda *_: (layer,) + (0,) * len(shape), pipeline_mode=pl.Buffered(1))


def _rms_scale(x):
    return lax.rsqrt(jnp.mean(x * x, axis=-1, keepdims=True) + EPS)


def _sigmoid(x):
    return 0.5 * jnp.tanh(0.5 * x) + 0.5


def _norm_matmul_kernel(x_ref, g_ref, w_ref, o_ref, h_ref):
    @pl.when(pl.program_id(1) == 0)
    def _():
        x = x_ref[...]
        h_ref[...] = (x * _rms_scale(x) * g_ref[...]).astype(BF16)

    o_ref[...] = jnp.dot(h_ref[...], w_ref[...], preferred_element_type=F32).astype(o_ref.dtype)


def norm_matmul(x, g, w, layer, *, out_dtype, tm, tn, name):
    m, d = x.shape
    n = w.shape[2]
    return pl.pallas_call(
        _norm_matmul_kernel,
        grid=(m // tm, n // tn),
        in_specs=[
            pl.BlockSpec((tm, d), lambda i, j: (i, 0)),
            pl.BlockSpec((1, d), lambda i, j: (0, 0)),
            pl.BlockSpec((None, d, tn), lambda i, j: (layer, 0, j)),
        ],
        out_specs=pl.BlockSpec((tm, tn), lambda i, j: (i, j)),
        out_shape=jax.ShapeDtypeStruct((m, n), out_dtype),
        scratch_shapes=[pltpu.VMEM((tm, d), BF16)],
        compiler_params=_params("parallel", "arbitrary"),
        name=name,
    )(x, g.reshape(1, d), w)


def _qk_proj_kernel(x_ref, g_ref, w_ref, cos_ref, sin_ref, gq_ref, gk_ref, qk_ref, h_ref, *, n_q_heads, q_scale):
    j = pl.program_id(1)

    @pl.when(j == 0)
    def _():
        x = x_ref[...]
        h_ref[...] = (x * _rms_scale(x) * g_ref[...]).astype(BF16)

    y = jnp.dot(h_ref[...], w_ref[...], preferred_element_type=F32)
    heads = y.shape[1] // HEAD_DIM
    gq = gq_ref[...] * q_scale
    gk = gk_ref[...]
    cos = cos_ref[...]
    sin = sin_ref[...]
    lane = lax.broadcasted_iota(jnp.int32, cos.shape, 1)
    first_half = (lane % (HEAD_DIM // 2)) < (HEAD_DIM // 4)
    mean_mat = jnp.full((HEAD_DIM, HEAD_DIM), 1.0 / HEAD_DIM, BF16)
    for h in range(heads):
        sl = slice(h * HEAD_DIM, (h + 1) * HEAD_DIM)
        g = jnp.where(j * heads + h < n_q_heads, gq, gk)
        xh = y[:, sl]
        sq = xh * xh
        sq_hi = sq.astype(BF16)
        sq_lo = (sq - sq_hi.astype(F32)).astype(BF16)
        ms = (jnp.dot(sq_hi, mean_mat, preferred_element_type=F32)
              + jnp.dot(sq_lo, mean_mat, preferred_element_type=F32))
        xn = xh * lax.rsqrt(ms + EPS) * g
        swapped = jnp.where(first_half,
                            pltpu.roll(xn, HEAD_DIM - HEAD_DIM // 4, axis=1),
                            pltpu.roll(xn, HEAD_DIM // 4, axis=1))
        qk_ref[:, sl] = (xn * cos + swapped * sin).astype(qk_ref.dtype)


def qk_proj(x, g, w_qk, layer, cos_t, sin_t, gq, gk, *, seq, attn_w, n, tm, tn, name):
    m, d = x.shape
    assert n % tn == 0 and tn % HEAD_DIM == 0 and seq % tm == 0
    t_blocks = seq // tm
    return pl.pallas_call(
        functools.partial(_qk_proj_kernel, n_q_heads=attn_w // HEAD_DIM, q_scale=HEAD_DIM ** -0.5 * LOG2E),
        grid=(m // tm, n // tn),
        in_specs=[
            pl.BlockSpec((tm, d), lambda i, j: (i, 0)),
            pl.BlockSpec((1, d), lambda i, j: (0, 0)),
            pl.BlockSpec((None, d, tn), lambda i, j: (layer, 0, j)),
            pl.BlockSpec((tm, HEAD_DIM), lambda i, j: (i % t_blocks, 0)),
            pl.BlockSpec((tm, HEAD_DIM), lambda i, j: (i % t_blocks, 0)),
            pl.BlockSpec((1, HEAD_DIM), lambda i, j: (0, 0)),
            pl.BlockSpec((1, HEAD_DIM), lambda i, j: (0, 0)),
        ],
        out_specs=[pl.BlockSpec((tm, tn), lambda i, j: (i, j)),
                   pl.BlockSpec((tm, d), lambda i, j: (i, 0))],
        out_shape=[jax.ShapeDtypeStruct((m, n), BF16), jax.ShapeDtypeStruct((m, d), BF16)],
        compiler_params=_params("parallel", "arbitrary"),
        name=name,
    )(x, g.reshape(1, d), w_qk, cos_t, sin_t, gq.reshape(1, HEAD_DIM), gk.reshape(1, HEAD_DIM))


def _matmul_kernel(a_ref, w_ref, o_ref):
    o_ref[...] = jnp.dot(a_ref[...], w_ref[...], preferred_element_type=F32).astype(o_ref.dtype)


def matmul(a, w, layer, *, col0, out_dtype, tm, tn, name):
    m, k = a.shape
    n = w.shape[2] - col0
    assert m % tm == 0 and n % tn == 0 and col0 % tn == 0
    blk0 = col0 // tn
    return pl.pallas_call(
        _matmul_kernel,
        grid=(m // tm, n // tn),
        in_specs=[pl.BlockSpec((tm, k), lambda i, j: (i, 0)),
                  pl.BlockSpec((None, k, tn), lambda i, j: (layer, 0, blk0 + j))],
        out_specs=pl.BlockSpec((tm, tn), lambda i, j: (i, j)),
        out_shape=jax.ShapeDtypeStruct((m, n), out_dtype),
        compiler_params=_params("parallel", "arbitrary"),
        name=name,
    )(a, w)


def rope_tables(seq):
    rows_n = seq // GRID_W
    row = jnp.repeat(jnp.arange(rows_n, dtype=F32), GRID_W)
    col = jnp.tile(jnp.arange(GRID_W, dtype=F32), rows_n)
    n_freq = HEAD_DIM // 4
    inv = ROPE_THETA ** (-jnp.arange(n_freq, dtype=F32) / n_freq)
    ang_r = row[:, None] * inv[None, :]
    ang_c = col[:, None] * inv[None, :]
    cos_t = jnp.concatenate([jnp.cos(ang_r), jnp.cos(ang_r), jnp.cos(ang_c), jnp.cos(ang_c)], axis=-1)
    sin_t = jnp.concatenate([-jnp.sin(ang_r), jnp.sin(ang_r), -jnp.sin(ang_c), jnp.sin(ang_c)], axis=-1)
    return cos_t, sin_t


def _qk_phase(q_ref, k_ref, s_ref, m_ref, t, *, tq, kc, racc):
    seq = k_ref.shape[0]
    n = KV_GROUP * tq
    rows_q = pl.ds(pl.multiple_of(t * tq, tq), tq)
    q = jnp.concatenate([q_ref[rows_q, g * HEAD_DIM:(g + 1) * HEAD_DIM] for g in range(KV_GROUP)], axis=0)
    m_acc = jnp.full((racc, n), -jnp.inf, F32)
    for c in range(seq // kc):
        st = lax.dot_general(k_ref[c * kc:(c + 1) * kc, :], q, (((1,), (1,)), ((), ())),
                             preferred_element_type=F32)
        s_ref[c * kc:(c + 1) * kc, :] = st
        for r in range(kc // racc):
            m_acc = jnp.maximum(m_acc, st[r * racc:(r + 1) * racc, :])
    m_ref[...] = jnp.broadcast_to(jnp.max(m_acc, axis=0, keepdims=True), m_ref.shape)


def _pv_phase(s_ref, m_ref, vt_ref, o_ref, t, *, tq, kc, racc):
    seq = s_ref.shape[0]
    n = KV_GROUP * tq
    rows_q = pl.ds(pl.multiple_of(t * tq, tq), tq)
    m = m_ref[0:1, :]
    l_acc = jnp.zeros((racc, n), F32)
    acc = jnp.zeros((HEAD_DIM, n), F32)
    for c in range(seq // kc):
        p = jnp.exp2(s_ref[c * kc:(c + 1) * kc, :] - m)
        for r in range(kc // racc):
            l_acc = l_acc + p[r * racc:(r + 1) * racc, :]
        acc = acc + jnp.dot(vt_ref[:, c * kc:(c + 1) * kc], p.astype(BF16), preferred_element_type=F32)
    out = acc * (1.0 / jnp.sum(l_acc, axis=0, keepdims=True))
    for g in range(KV_GROUP):
        o_ref[rows_q, g * HEAD_DIM:(g + 1) * HEAD_DIM] = out[:, g * tq:(g + 1) * tq].T.astype(o_ref.dtype)


def _attn_kernel(q_ref, k_ref, v_ref, o_ref, s0_ref, s1_ref, m0_ref, m1_ref, vt_ref, *, tq, kc, racc):
    seq = k_ref.shape[0]
    nq = seq // tq
    for c in range(seq // kc):
        vt_ref[:, c * kc:(c + 1) * kc] = v_ref[c * kc:(c + 1) * kc, :].astype(F32).T.astype(BF16)
    qk = functools.partial(_qk_phase, q_ref, k_ref, tq=tq, kc=kc, racc=racc)
    pv = functools.partial(_pv_phase, vt_ref=vt_ref, o_ref=o_ref, tq=tq, kc=kc, racc=racc)
    qk(s0_ref, m0_ref, 0)

    def pair(i, carry):
        t = 2 * i
        qk(s1_ref, m1_ref, t + 1)
        pv(s0_ref, m0_ref, t=t)
        qk(s0_ref, m0_ref, t + 2)
        pv(s1_ref, m1_ref, t=t + 1)
        return carry

    lax.fori_loop(0, nq // 2 - 1, pair, 0)
    qk(s1_ref, m1_ref, nq - 1)
    pv(s0_ref, m0_ref, t=nq - 2)
    pv(s1_ref, m1_ref, t=nq - 1)


def gqa_attention(qk, rest, *, batch, seq, n_q_cols, v_col, tq, kc, racc, name):
    n_kv = n_q_cols // (KV_GROUP * HEAD_DIM)
    group_w = KV_GROUP * HEAD_DIM
    qk3 = qk.reshape(batch, seq, qk.shape[1])
    v3 = rest.reshape(batch, seq, rest.shape[1])
    k_blk0 = n_q_cols // HEAD_DIM
    v_blk0 = v_col // HEAD_DIM
    n = KV_GROUP * tq
    assert (seq // tq) % 2 == 0 and seq % kc == 0 and kc % racc == 0
    out = pl.pallas_call(
        functools.partial(_attn_kernel, tq=tq, kc=kc, racc=racc),
        grid=(batch, n_kv),
        in_specs=[
            pl.BlockSpec((None, seq, group_w), lambda b, j: (b, 0, j)),
            pl.BlockSpec((None, seq, HEAD_DIM), lambda b, j: (b, 0, k_blk0 + j)),
            pl.BlockSpec((None, seq, HEAD_DIM), lambda b, j: (b, 0, v_blk0 + j)),
        ],
        out_specs=pl.BlockSpec((None, seq, group_w), lambda b, j: (b, 0, j)),
        out_shape=jax.ShapeDtypeStruct((batch, seq, n_q_cols), BF16),
        scratch_shapes=[pltpu.VMEM((seq, n), F32), pltpu.VMEM((seq, n), F32),
                        pltpu.VMEM((SUBLANES, n), F32), pltpu.VMEM((SUBLANES, n), F32),
                        pltpu.VMEM((HEAD_DIM, seq), BF16)],
        compiler_params=_params("parallel", "parallel"),
        name=name,
    )(qk3, qk3, v3)
    return out.reshape(batch * seq, n_q_cols)


def _softplus(x):
    e = jnp.exp(-jnp.abs(x))
    u = 1.0 + e
    log1p_e = jnp.where(u == 1.0, e, jnp.log(u) * e / (u - 1.0))
    return jnp.maximum(x, 0.0) + log1p_e


def _scan8(a, b, reverse):
    row = lax.broadcasted_iota(jnp.int32, a.shape, 0)
    for d in (1, 2, 4):
        if reverse:
            valid = row < SUBLANES - d
            shift = SUBLANES - d
        else:
            valid = row >= d
            shift = d
        a_prev = jnp.where(valid, pltpu.roll(a, shift, axis=0), 1.0)
        b_prev = jnp.where(valid, pltpu.roll(b, shift, axis=0), 0.0)
        b = a * b_prev + b
        a = a * a_prev
    return a, b


SQRT_TINY = 1e-30
PITCH_PAD = SUBLANES


def _shift_rows(x, down):
    row = lax.broadcasted_iota(jnp.int32, x.shape, 0)
    if down:
        return jnp.where(row >= 1, pltpu.roll(x, 1, axis=0), 0.0)
    return jnp.where(row < SUBLANES - 1, pltpu.roll(x, SUBLANES - 1, axis=0), 0.0)


def _lru_kernel(u_ref, y_ref, cw_ref, cb_ref, wg_ref, bg_ref, lam_ref, o_ref,
                un_ref, ui_ref, af_ref, bf_ref, ab_ref, bb_ref, pf_ref, hf_ref, pb_ref, hb_ref, hn_ref,
                *, tc, unroll):
    seq, cb = u_ref.shape
    nt = seq // SUBLANES
    pitch = nt + PITCH_PAD
    halo = CONV_LEFT

    for s in range(SUBLANES):
        un_ref[s * pitch:s * pitch + nt, :] = u_ref[s * nt:(s + 1) * nt, :].astype(F32)

    def gather(i, carry):
        t0 = i * unroll
        tiles = [un_ref[pl.ds(t0 + k, SUBLANES, stride=pitch), :] for k in range(unroll)]
        ui_ref[pl.ds(pl.multiple_of((t0 + halo) * SUBLANES, SUBLANES), unroll * SUBLANES), :] = \
            jnp.concatenate(tiles, axis=0)
        return carry

    lax.fori_loop(0, nt // unroll, gather, 0)
    for k in range(halo):
        src = (nt + k) * SUBLANES
        ui_ref[k * SUBLANES:(k + 1) * SUBLANES, :] = _shift_rows(ui_ref[src:src + SUBLANES, :], down=True)
    for k in range(CONV_W - 1 - CONV_LEFT):
        src = (halo + k) * SUBLANES
        dst = (halo + nt + k) * SUBLANES
        ui_ref[dst:dst + SUBLANES, :] = _shift_rows(ui_ref[src:src + SUBLANES, :], down=False)

    wg = wg_ref[...] * 0.5
    bg = bg_ref[...] * 0.5
    bg_hi = bg.astype(BF16).astype(F32)
    bg_lo = (bg - bg_hi).astype(BF16).astype(F32)
    w_row = lax.broadcasted_iota(jnp.int32, wg.shape, 0)
    w_bias = jnp.where(w_row == 0, bg_hi, jnp.where(w_row == 1, bg_lo, 0.0)).astype(BF16)
    w_aug = jnp.concatenate([wg, w_bias], axis=0)
    ones_cols = jnp.where(lax.broadcasted_iota(jnp.int32, (tc, cb), 1) < 2, 1.0, 0.0).astype(BF16)
    half_decay = [(-0.5 * LRU_C * LOG2E) * _softplus(-lam_ref[d:d + 1, :]) for d in range(2)]

    for c in range(seq // tc):
        r0 = c * tc
        uf = jnp.broadcast_to(cb_ref[...], (tc, cb))
        for tap in range(CONV_W):
            uf = uf + ui_ref[r0 + tap * SUBLANES:r0 + tap * SUBLANES + tc, :] * cw_ref[tap:tap + 1, :]
        gates = jnp.dot(jnp.concatenate([uf.astype(BF16), ones_cols], axis=1), w_aug,
                        preferred_element_type=F32)
        uh = 0.5 * uf
        for d, (a_ref, b_ref) in enumerate(((af_ref, bf_ref), (ab_ref, bb_ref))):
            tr = jnp.tanh(gates[:, (2 * d) * cb:(2 * d + 1) * cb])
            ti = jnp.tanh(gates[:, (2 * d + 1) * cb:(2 * d + 2) * cb])
            a = jnp.exp2(half_decay[d] * tr + half_decay[d])
            iu = uh * ti + uh
            om = 1.0 - a * a
            a_ref[r0:r0 + tc, :] = a
            b_ref[r0:r0 + tc, :] = (om * lax.rsqrt(jnp.maximum(om, SQRT_TINY))) * iu

    nh = nt // 2

    blk = unroll
    blk_rows = blk * SUBLANES

    def window(t0):
        return pl.ds(pl.multiple_of(t0 * SUBLANES, blk_rows), blk_rows)

    def chain(a_ref, b_ref, p_ref, h_ref, t0, h, p, reverse):
        r = window(t0)
        a_blk = a_ref[r, :]
        b_blk = b_ref[r, :]
        hs, ps = [None] * blk, [None] * blk
        for k in (range(blk - 1, -1, -1) if reverse else range(blk)):
            a = a_blk[k * SUBLANES:(k + 1) * SUBLANES, :]
            h = a * h + b_blk[k * SUBLANES:(k + 1) * SUBLANES, :]
            p = a * p
            hs[k], ps[k] = h, p
        h_ref[r, :] = jnp.concatenate(hs, axis=0)
        p_ref[r, :] = jnp.concatenate(ps, axis=0)
        return h, p

    def scan_body(i, carry):
        hf0, pf0, hf1, pf1, hb1, pb1, hb0, pb0 = carry
        t0 = i * blk
        hf0, pf0 = chain(af_ref, bf_ref, pf_ref, hf_ref, t0, hf0, pf0, False)
        hf1, pf1 = chain(af_ref, bf_ref, pf_ref, hf_ref, nh + t0, hf1, pf1, False)
        hb1, pb1 = chain(ab_ref, bb_ref, pb_ref, hb_ref, nt - blk - t0, hb1, pb1, True)
        hb0, pb0 = chain(ab_ref, bb_ref, pb_ref, hb_ref, nh - blk - t0, hb0, pb0, True)
        return hf0, pf0, hf1, pf1, hb1, pb1, hb0, pb0

    zero = jnp.zeros((SUBLANES, cb), F32)
    one = jnp.ones((SUBLANES, cb), F32)
    hf0, pf0, hf1, pf1, hb1, pb1, hb0, pb0 = lax.fori_loop(0, nh // blk, scan_body, (zero, one) * 4)
    init_f0 = _shift_rows(_scan8(pf0 * pf1, pf1 * hf0 + hf1, reverse=False)[1], down=True)
    init_f1 = pf0 * init_f0 + hf0
    init_b1 = _shift_rows(_scan8(pb1 * pb0, pb0 * hb1 + hb0, reverse=True)[1], down=False)
    init_b0 = pb1 * init_b1 + hb1

    def fix(i, carry):
        for t0, init_f, init_b in ((i * blk, init_f0, init_b0), (nh + i * blk, init_f1, init_b1)):
            r = window(t0)
            hf_blk, pf_blk, hb_blk, pb_blk = hf_ref[r, :], pf_ref[r, :], hb_ref[r, :], pb_ref[r, :]
            for k in range(blk):
                sl = slice(k * SUBLANES, (k + 1) * SUBLANES)
                h = (hf_blk[sl, :] + pf_blk[sl, :] * init_f) + (hb_blk[sl, :] + pb_blk[sl, :] * init_b)
                hn_ref[pl.ds(t0 + k, SUBLANES, stride=pitch), :] = h
        return carry

    lax.fori_loop(0, nh // blk, fix, 0)

    for s in range(SUBLANES):
        for c in range(nt // tc):
            src = slice(s * pitch + c * tc, s * pitch + (c + 1) * tc)
            dst = slice(s * nt + c * tc, s * nt + (c + 1) * tc)
            yv = y_ref[dst, :].astype(F32)
            th = jnp.tanh(yv * (GELU_C0 + GELU_C1 * (yv * yv)))
            half_y = 0.5 * yv
            o_ref[dst, :] = (hn_ref[src, :] * (half_y * th + half_y)).astype(o_ref.dtype)


def rglru_branch(rest, conv_w, conv_b, w_gates, b_gates, lam, *, batch, seq, d_rnn, u_col, y_col, name):
    cb = LANES
    n_blk = d_rnn // cb
    rest3 = rest.reshape(batch, seq, rest.shape[1])
    u_blk0, y_blk0 = u_col // cb, y_col // cb
    nt = seq // SUBLANES
    padded = SUBLANES * (nt + PITCH_PAD)
    tc = 256
    assert nt % tc == 0
    out = pl.pallas_call(
        functools.partial(_lru_kernel, tc=tc, unroll=8),
        grid=(batch, n_blk),
        in_specs=[
            pl.BlockSpec((None, seq, cb), lambda b, n: (b, 0, u_blk0 + n)),
            pl.BlockSpec((None, seq, cb), lambda b, n: (b, 0, y_blk0 + n)),
            pl.BlockSpec((CONV_W, cb), lambda b, n: (0, n)),
            pl.BlockSpec((1, cb), lambda b, n: (0, n)),
            pl.BlockSpec((None, cb, 4 * cb), lambda b, n: (n, 0, 0)),
            pl.BlockSpec((None, 1, 4 * cb), lambda b, n: (n, 0, 0)),
            pl.BlockSpec((2, cb), lambda b, n: (0, n)),
        ],
        out_specs=pl.BlockSpec((None, seq, cb), lambda b, n: (b, 0, n)),
        out_shape=jax.ShapeDtypeStruct((batch, seq, d_rnn), BF16),
        scratch_shapes=[pltpu.VMEM((padded, cb), F32),
                        pltpu.VMEM((seq + (CONV_W - 1) * SUBLANES, cb), F32)]
                       + [pltpu.VMEM((seq, cb), F32)] * 8 + [pltpu.VMEM((padded, cb), F32)],
        compiler_params=_params("parallel", "parallel"),
        name=name,
    )(rest3, rest3, conv_w, conv_b.reshape(1, d_rnn), w_gates, b_gates, lam)
    return out.reshape(batch * seq, d_rnn)


def _merge_mix_kernel(oa_ref, or_ref, x_ref, wa_ref, wr_ref, wo_ref, *rest, nc):
    d = wo_ref.shape[1]
    n_chunks = d // nc
    ga_refs, gr_refs = rest[:n_chunks], rest[n_chunks:2 * n_chunks]
    o_ref, mg_ref = rest[2 * n_chunks:]
    oa = oa_ref[...]
    orn = or_ref[...]
    for c in range(n_chunks):
        sl = slice(c * nc, (c + 1) * nc)
        ya = jnp.dot(oa, wa_ref[:, sl], preferred_element_type=F32)
        yr = jnp.dot(orn, wr_ref[:, sl], preferred_element_type=F32)
        mg_ref[:, sl] = (_sigmoid(ga_refs[c][...].astype(F32)) * ya
                         + _sigmoid(gr_refs[c][...].astype(F32)) * yr).astype(BF16)
    mg = mg_ref[...]
    for c in range(n_chunks):
        sl = slice(c * nc, (c + 1) * nc)
        o_ref[:, sl] = x_ref[:, sl] + jnp.dot(mg, wo_ref[:, sl], preferred_element_type=F32)


def merge_mix(o_attn, o_rnn, gates, x, w_a, w_r, w_o, layer, *, ga_col, gr_col, tm, nc, name):
    m, d = x.shape
    assert ga_col % nc == 0 and gr_col % nc == 0 and w_a.shape[1:] == w_r.shape[1:] == w_o.shape[1:] == (d, d)
    row = lambda i: (i, 0)
    gate_specs = [pl.BlockSpec((tm, nc), functools.partial(lambda i, blk: (i, blk), blk=col // nc + c))
                  for col in (ga_col, gr_col) for c in range(d // nc)]
    return pl.pallas_call(
        functools.partial(_merge_mix_kernel, nc=nc),
        grid=(m // tm,),
        in_specs=[
            pl.BlockSpec((tm, d), row),
            pl.BlockSpec((tm, d), row),
            pl.BlockSpec((tm, d), row),
            _resident_layer((d, d), layer), _resident_layer((d, d), layer), _resident_layer((d, d), layer),
        ] + gate_specs,
        out_specs=pl.BlockSpec((tm, d), row),
        out_shape=jax.ShapeDtypeStruct((m, d), F32),
        scratch_shapes=[pltpu.VMEM((tm, d), BF16)],
        compiler_params=_params("parallel"),
        name=name,
    )(o_attn, o_rnn, x, w_a, w_r, w_o, *([gates] * (2 * (d // nc))))


def _cross_kernel(x_ref, g_ref, kv_ref, wq_ref, wo_ref, o_ref, q_ref, a_ref, *, n_heads, nc):
    d = x_ref.shape[1]
    hd = d // n_heads
    scale = hd ** -0.5 * LOG2E
    x = x_ref[...]
    hc = (x * _rms_scale(x) * g_ref[...]).astype(BF16)
    for c in range(d // nc):
        sl = slice(c * nc, (c + 1) * nc)
        q_ref[:, sl] = jnp.dot(hc, wq_ref[:, sl], preferred_element_type=F32).astype(BF16)
    for h in range(n_heads):
        sl = slice(h * hd, (h + 1) * hd)
        k = kv_ref[:, h * hd:(h + 1) * hd]
        v = kv_ref[:, d + h * hd:d + (h + 1) * hd]
        s = lax.dot_general(q_ref[:, sl], k, (((1,), (1,)), ((), ())), preferred_element_type=F32) * scale
        p = jnp.exp2(s - jnp.max(s, axis=-1, keepdims=True))
        o = jnp.dot(p.astype(BF16), v, preferred_element_type=F32) / jnp.sum(p, axis=-1, keepdims=True)
        a_ref[:, sl] = o.astype(BF16)
    a = a_ref[...]
    for c in range(d // nc):
        sl = slice(c * nc, (c + 1) * nc)
        o_ref[:, sl] = x_ref[:, sl] + jnp.dot(a, wo_ref[:, sl], preferred_element_type=F32)


def cross_block(x, g, kv, w_q, w_o, layer, *, batch, seq, tq, nc, name):
    d = x.shape[1]
    n_mem = kv.shape[0] // batch
    out = pl.pallas_call(
        functools.partial(_cross_kernel, n_heads=N_XHEADS, nc=nc),
        grid=(batch, seq // tq),
        in_specs=[
            pl.BlockSpec((None, tq, d), lambda b, i: (b, i, 0)),
            pl.BlockSpec((1, d), lambda b, i: (0, 0)),
            pl.BlockSpec((None, n_mem, 2 * d), lambda b, i: (b, 0, 0)),
            _resident_layer((d, d), layer), _resident_layer((d, d), layer),
        ],
        out_specs=pl.BlockSpec((None, tq, d), lambda b, i: (b, i, 0)),
        out_shape=jax.ShapeDtypeStruct((batch, seq, d), F32),
        scratch_shapes=[pltpu.VMEM((tq, d), BF16), pltpu.VMEM((tq, d), BF16)],
        compiler_params=_params("parallel", "parallel"),
        name=name,
    )(x.reshape(batch, seq, d), g.reshape(1, d), kv.reshape(batch, n_mem, 2 * d), w_q, w_o)
    return out.reshape(batch * seq, d)


def _mlp_kernel(x_ref, g_ref, wu_ref, wd_ref, gf_ref, o_ref, h_ref, *, nc, final_norm):
    f = pl.program_id(1)
    d = o_ref.shape[1]

    @pl.when(f == 0)
    def _():
        x = x_ref[...]
        h_ref[...] = (x * _rms_scale(x) * g_ref[...]).astype(BF16)
        o_ref[...] = x

    up = jnp.dot(h_ref[...], wu_ref[...], preferred_element_type=F32)
    act = jnp.square(jnp.maximum(up, 0.0)).astype(BF16)
    for c in range(d // nc):
        sl = slice(c * nc, (c + 1) * nc)
        o_ref[:, sl] += jnp.dot(act, wd_ref[:, sl], preferred_element_type=F32)

    if final_norm:
        @pl.when(f == pl.num_programs(1) - 1)
        def _():
            y = o_ref[...]
            o_ref[...] = y * _rms_scale(y) * gf_ref[...]


def mlp_block(x, g, w_up, w_down, layer, g_final, *, tm, tf, nc, final_norm, name):
    m, d = x.shape
    d_ff = w_up.shape[2]
    return pl.pallas_call(
        functools.partial(_mlp_kernel, nc=nc, final_norm=final_norm),
        grid=(m // tm, d_ff // tf),
        in_specs=[
            pl.BlockSpec((tm, d), lambda i, f: (i, 0)),
            pl.BlockSpec((1, d), lambda i, f: (0, 0)),
            pl.BlockSpec((None, d, tf), lambda i, f: (layer, 0, f)),
            pl.BlockSpec((None, tf, d), lambda i, f: (layer, f, 0)),
            pl.BlockSpec((1, d), lambda i, f: (0, 0)),
        ],
        out_specs=pl.BlockSpec((tm, d), lambda i, f: (i, 0)),
        out_shape=jax.ShapeDtypeStruct((m, d), F32),
        scratch_shapes=[pltpu.VMEM((tm, d), BF16)],
        compiler_params=_params("parallel", "arbitrary"),
        name=name,
    )(x, g.reshape(1, d), w_up, w_down, g_final.reshape(1, d))


def kernel(x, mem, mix_norm_g, w_in, q_norm_g, k_norm_g, conv_w, conv_b, lru_w_r, lru_b_r, lru_w_i, lru_b_i, lru_lambda, w_attn_branch, w_rnn_branch, w_mix_out, cross_norm_g, mem_norm_g, w_xq, w_xkv, w_xo, mlp_norm_g, w_up, w_down, final_norm_g):
    batch, seq, d_model = x.shape
    n_mem = mem.shape[1]
    depth = w_in.shape[0]
    attn_w = w_attn_branch.shape[1]
    d_rnn = w_rnn_branch.shape[1]
    kv_w = attn_w // KV_GROUP
    n_blocks, rnn_block = lru_w_r.shape[2], lru_w_r.shape[3]
    assert w_in.shape[2] == attn_w + 2 * kv_w + 2 * d_rnn + 2 * d_model and rnn_block == LANES and depth >= 1

    xf = x.reshape(batch * seq, d_model)
    memf = mem.reshape(batch * n_mem, d_model)
    cos_t, sin_t = rope_tables(seq)

    qk_w = attn_w + kv_w
    w_in_b = w_in.astype(BF16)
    v_col, u_col = 0, kv_w
    y_col = u_col + d_rnn
    ga_col = y_col + d_rnn
    gr_col = ga_col + d_model
    w_a, w_r, w_o = (w.astype(BF16) for w in (w_attn_branch, w_rnn_branch, w_mix_out))
    w_xq_b, w_xkv_b, w_xo_b, w_up_b, w_down_b = (w.astype(BF16) for w in (w_xq, w_xkv, w_xo, w_up, w_down))

    for l in range(depth):
        tag = f"l{l}_"
        qk, h = qk_proj(xf, mix_norm_g[l], w_in_b, l, cos_t, sin_t, q_norm_g[l], k_norm_g[l], seq=seq, attn_w=attn_w,
                        n=qk_w, tm=1024, tn=qk_w // 2, name=tag + "qk_proj")
        rest = matmul(h, w_in_b, l, col0=qk_w, out_dtype=BF16, tm=2048, tn=512, name=tag + "rest_proj")
        o_attn = gqa_attention(qk, rest, batch=batch, seq=seq, n_q_cols=attn_w, v_col=v_col, tq=64, kc=512, racc=32,
                               name=tag + "attn")
        w_gates = jnp.concatenate([lru_w_r[l, 0], lru_w_i[l, 0], lru_w_r[l, 1], lru_w_i[l, 1]],
                                  axis=-1).astype(BF16)
        b_gates = jnp.concatenate(
            [b.reshape(n_blocks, 1, rnn_block) for b in (lru_b_r[l, 0], lru_b_i[l, 0], lru_b_r[l, 1], lru_b_i[l, 1])],
            axis=-1)
        o_rnn = rglru_branch(rest, conv_w[l], conv_b[l], w_gates, b_gates, lru_lambda[l], batch=batch, seq=seq,
                             d_rnn=d_rnn, u_col=u_col, y_col=y_col, name=tag + "rglru")
        xf = merge_mix(o_attn, o_rnn, rest, xf, w_a, w_r, w_o, l, ga_col=ga_col, gr_col=gr_col, tm=256, nc=512,
                       name=tag + "merge_mix")
        kv = norm_matmul(memf, mem_norm_g[l], w_xkv_b, l, out_dtype=BF16, tm=batch * n_mem, tn=1024,
                         name=tag + "xkv")
        xf = cross_block(xf, cross_norm_g[l], kv, w_xq_b, w_xo_b, l, batch=batch, seq=seq, tq=512, nc=512,
                         name=tag + "cross")
        xf = mlp_block(xf, mlp_norm_g[l], w_up_b, w_down_b, l, final_norm_g, tm=1024, tf=512, nc=512,
                       final_norm=(l == depth - 1), name=tag + "mlp")
    return xf.reshape(batch, seq, d_model)
```
